```python
import jax, jax.numpy as jnp
from jax import lax
import numpy as np

D_MODEL = 2048
BATCH = 8
SEQ = 4096
DEPTH = 4

GRID_W = 64
N_MIXERS = 2
NORM_EPS = 1e-6

SSM_EXPAND = 2
D_INNER = SSM_EXPAND * D_MODEL
SSM_HEAD_DIM = 64
SSM_HEADS = D_INNER // SSM_HEAD_DIM
SSM_GROUPS = 8
SSM_STATE = 128
SSM_CONV_W = 5
SSM_CHUNK = 128
CONV_DIM = D_INNER + 2 * SSM_GROUPS * SSM_STATE
SSM_IN_DIM = D_INNER + CONV_DIM + 2 * SSM_HEADS

ATT_HEAD_DIM = 128
ATT_HEADS = D_MODEL // ATT_HEAD_DIM
ATT_KV_HEADS = 4
ATT_REP = ATT_HEADS // ATT_KV_HEADS
ATT_Q_BLOCK = 128
ROPE_THETA = 10000.0
ROPE_AXIS_DIM = ATT_HEAD_DIM // 2
QKV_DIM = (ATT_HEADS + 2 * ATT_KV_HEADS) * ATT_HEAD_DIM

D_FF = 5632
FFN_CONV_W = 3

N_SSM_LAYERS = (DEPTH + 1) // N_MIXERS
N_ATT_LAYERS = DEPTH // N_MIXERS

kernel_name = "hybrid_ssd_axial_gqa_convffn_encoder"


def rms_norm(x, w):
    xf = x.astype(jnp.float32)
    y = xf * lax.rsqrt(jnp.mean(xf * xf, axis=-1, keepdims=True) + NORM_EPS)
    return (y * w.astype(jnp.float32)).astype(x.dtype)


def depthwise_conv_centred(x, w, b):
    pad = w.shape[0] // 2
    y = lax.conv_general_dilated(
        x, w[:, None, :].astype(x.dtype), window_strides=(1,), padding=[(pad, pad)],
        dimension_numbers=("NWC", "WIO", "NWC"), feature_group_count=x.shape[-1])
    return y + b.astype(x.dtype)


def segsum(a):
    t = a.shape[-1]
    cs = jnp.cumsum(a, axis=-1)
    diff = cs[..., :, None] - cs[..., None, :]
    mask = jnp.tril(jnp.ones((t, t), dtype=bool))
    return jnp.where(mask, diff, -jnp.inf)


def ssd_scan(x, dt, A, B, C):
    b, l, h, p = x.shape
    g, n = B.shape[2], B.shape[3]
    r = h // g
    T = SSM_CHUNK
    c = l // T
    xdt = (x * dt[..., None]).reshape(b, c, T, g, r, p)
    a = (dt * A).reshape(b, c, T, g, r).transpose(0, 3, 4, 1, 2)
    Bc = B.reshape(b, c, T, g, n)
    Cc = C.reshape(b, c, T, g, n)
    a_cum = jnp.cumsum(a, axis=-1)
    decay_in = jnp.exp(segsum(a))
    cb = jnp.einsum('bclgn,bcsgn->bcgls', Cc, Bc)
    y_diag = jnp.einsum('bcgls,bgrcls,bcsgrp->bclgrp', cb, decay_in, xdt)
    decay_states = jnp.exp(a_cum[..., -1:] - a_cum)
    states = jnp.einsum('bcsgn,bgrcs,bcsgrp->bcgrpn', Bc, decay_states, xdt)
    states = jnp.concatenate([jnp.zeros_like(states[:, :1]), states], axis=1)
    chunk_a = jnp.pad(a_cum[..., -1], ((0, 0), (0, 0), (0, 0), (1, 0)))
    chunk_decay = jnp.exp(segsum(chunk_a))
    new_states = jnp.einsum('bgrzc,bcgrpn->bzgrpn', chunk_decay, states)
    prev_states = new_states[:, :-1]
    y_off = jnp.einsum('bclgn,bcgrpn,bgrcl->bclgrp', Cc, prev_states, jnp.exp(a_cum))
    return (y_diag + y_off).reshape(b, l, h, p)


def ssd_mixer(hn, w_in, conv_w, conv_b, dt_bias, a_log, d_skip, norm_w, w_out):
    b, l, _ = hn.shape
    f32 = jnp.float32
    zxbcdt = hn @ w_in
    z, xbc, dt_raw = jnp.split(zxbcdt, [D_INNER, D_INNER + CONV_DIM], axis=-1)
    xbc = jax.nn.silu(depthwise_conv_centred(xbc, conv_w, conv_b))
    xs, Bm, Cm = jnp.split(xbc, [D_INNER, D_INNER + SSM_GROUPS * SSM_STATE], axis=-1)
    xs = xs.reshape(b, l, SSM_HEADS, SSM_HEAD_DIM).astype(f32)
    Bm = Bm.reshape(b, l, SSM_GROUPS, SSM_STATE).astype(f32)
    Cm = Cm.reshape(b, l, SSM_GROUPS, SSM_STATE).astype(f32)
    dt = jax.nn.softplus(dt_raw.astype(f32).reshape(b, l, 2, SSM_HEADS) + dt_bias.astype(f32))
    A = -jnp.exp(a_log.astype(f32))
    y_fwd = ssd_scan(xs, dt[:, :, 0], A[0], Bm, Cm)
    flip = lambda t: jnp.flip(t, axis=1)
    y_bwd = flip(ssd_scan(flip(xs), flip(dt[:, :, 1]), A[1], flip(Bm), flip(Cm)))
    y = y_fwd + y_bwd + xs * d_skip.astype(f32)[:, None]
    y = y.reshape(b, l, D_INNER) * jax.nn.silu(z.astype(f32))
    yg = y.reshape(b, l, SSM_GROUPS, D_INNER // SSM_GROUPS)
    yg = yg * lax.rsqrt(jnp.mean(yg * yg, axis=-1, keepdims=True) + NORM_EPS)
    y = yg.reshape(b, l, D_INNER) * norm_w.astype(f32)
    return y.astype(hn.dtype) @ w_out


def axial_rope_tables(l):
    rows = l // GRID_W
    row_id = jnp.repeat(jnp.arange(rows), GRID_W)
    col_id = jnp.arange(rows * GRID_W) % GRID_W
    half = ROPE_AXIS_DIM // 2
    inv_freq = ROPE_THETA ** (-jnp.arange(half, dtype=jnp.float32) / half)
    pos = jnp.stack([row_id, col_id], axis=-1).astype(jnp.float32)
    ang = pos[:, :, None] * inv_freq
    return jnp.cos(ang), jnp.sin(ang)


def apply_axial_rope(x, cos, sin):
    b, l, h, d = x.shape
    xr = x.reshape(b, l, h, 2, 2, d // 4)
    x1, x2 = xr[..., 0, :], xr[..., 1, :]
    c = cos[None, :, None]
    s = sin[None, :, None]
    out = jnp.stack([x1 * c - x2 * s, x2 * c + x1 * s], axis=-2)
    return out.reshape(b, l, h, d)


def head_rms_norm(x, w):
    xf = x.astype(jnp.float32)
    return xf * lax.rsqrt(jnp.mean(xf * xf, axis=-1, keepdims=True) + NORM_EPS) * w.astype(jnp.float32)


def gqa_mixer(hn, w_qkv, q_norm_w, k_norm_w, w_o, cos, sin):
    b, l, _ = hn.shape
    qkv = hn @ w_qkv
    q, k, v = jnp.split(qkv, [ATT_HEADS * ATT_HEAD_DIM, (ATT_HEADS + ATT_KV_HEADS) * ATT_HEAD_DIM], axis=-1)
    q = apply_axial_rope(head_rms_norm(q.reshape(b, l, ATT_HEADS, ATT_HEAD_DIM), q_norm_w), cos, sin)
    k = apply_axial_rope(head_rms_norm(k.reshape(b, l, ATT_KV_HEADS, ATT_HEAD_DIM), k_norm_w), cos, sin)
    v = v.reshape(b, l, ATT_KV_HEADS, ATT_HEAD_DIM)
    q = q * (ATT_HEAD_DIM ** -0.5)
    nblk = l // ATT_Q_BLOCK
    qb = q.reshape(b, nblk, ATT_Q_BLOCK, ATT_KV_HEADS, ATT_REP, ATT_HEAD_DIM).transpose(1, 0, 2, 3, 4, 5)

    def attend(q_blk):
        s = jnp.einsum('bqkrd,bskd->bkrqs', q_blk, k)
        p = jax.nn.softmax(s, axis=-1)
        return jnp.einsum('bkrqs,bskd->bqkrd', p.astype(v.dtype), v)

    o = lax.map(attend, qb)
    o = o.transpose(1, 0, 2, 3, 4, 5).reshape(b, l, ATT_HEADS * ATT_HEAD_DIM)
    return o.astype(hn.dtype) @ w_o


def conv_ffn(hn, w_up, conv_w, conv_b, w_down):
    u = depthwise_conv_centred(hn @ w_up, conv_w, conv_b)
    gate, val = jnp.split(u, 2, axis=-1)
    return (jax.nn.silu(gate) * val) @ w_down


def setup_inputs(seed: int = 0) -> dict:
    key = jax.random.key(seed)
    ks = jax.random.split(key, 20)
    f32 = jnp.float32
    nrm = lambda k, shape, scale: jax.random.normal(k, shape, f32) * scale
    dt0 = jnp.exp(jax.random.uniform(ks[6], (N_SSM_LAYERS, 2, SSM_HEADS), f32,
                                     jnp.log(1e-3), jnp.log(1e-1)))
    return {
        "x": nrm(ks[0], (BATCH, SEQ, D_MODEL), 1.0),
        "norm_mix_w": 1.0 + nrm(ks[1], (DEPTH, D_MODEL), 0.02),
        "norm_ffn_w": 1.0 + nrm(ks[2], (DEPTH, D_MODEL), 0.02),
        "ssm_w_in": nrm(ks[3], (N_SSM_LAYERS, D_MODEL, SSM_IN_DIM), D_MODEL ** -0.5),
        "ssm_conv_w": nrm(ks[4], (N_SSM_LAYERS, SSM_CONV_W, CONV_DIM), SSM_CONV_W ** -0.5),
        "ssm_conv_b": nrm(ks[5], (N_SSM_LAYERS, CONV_DIM), 0.02),
        "ssm_dt_bias": dt0 + jnp.log(-jnp.expm1(-dt0)),
        "ssm_a_log": jnp.log(jax.random.uniform(ks[7], (N_SSM_LAYERS, 2, SSM_HEADS), f32, 1.0, 16.0)),
        "ssm_d": 1.0 + nrm(ks[8], (N_SSM_LAYERS, SSM_HEADS), 0.02),
        "ssm_norm_w": 1.0 + nrm(ks[9], (N_SSM_LAYERS, D_INNER), 0.02),
        "ssm_w_out": nrm(ks[10], (N_SSM_LAYERS, D_INNER, D_MODEL), D_INNER ** -0.5),
        "att_w_qkv": nrm(ks[11], (N_ATT_LAYERS, D_MODEL, QKV_DIM), D_MODEL ** -0.5),
        "att_q_norm_w": 1.0 + nrm(ks[12], (N_ATT_LAYERS, ATT_HEAD_DIM), 0.02),
        "att_k_norm_w": 1.0 + nrm(ks[13], (N_ATT_LAYERS, ATT_HEAD_DIM), 0.02),
        "att_w_o": nrm(ks[14], (N_ATT_LAYERS, ATT_HEADS * ATT_HEAD_DIM, D_MODEL), D_MODEL ** -0.5),
        "ffn_w_up": nrm(ks[15], (DEPTH, D_MODEL, 2 * D_FF), D_MODEL ** -0.5),
        "ffn_conv_w": nrm(ks[16], (DEPTH, FFN_CONV_W, 2 * D_FF), FFN_CONV_W ** -0.5),
        "ffn_conv_b": nrm(ks[17], (DEPTH, 2 * D_FF), 0.02),
        "ffn_w_down": nrm(ks[18], (DEPTH, D_FF, D_MODEL), D_FF ** -0.5),
    }


def reference(x, norm_mix_w, norm_ffn_w, ssm_w_in, ssm_conv_w, ssm_conv_b, ssm_dt_bias, ssm_a_log,
              ssm_d, ssm_norm_w, ssm_w_out, att_w_qkv, att_q_norm_w, att_k_norm_w, att_w_o,
              ffn_w_up, ffn_conv_w, ffn_conv_b, ffn_w_down):
    l = x.shape[1]
    cos, sin = axial_rope_tables(l)
    h = x
    for i in range(DEPTH):
        hn = rms_norm(h, norm_mix_w[i])
        j = i // N_MIXERS
        if i % N_MIXERS == 0:
            mix = ssd_mixer(hn, ssm_w_in[j], ssm_conv_w[j], ssm_conv_b[j], ssm_dt_bias[j],
                            ssm_a_log[j], ssm_d[j], ssm_norm_w[j], ssm_w_out[j])
        else:
            mix = gqa_mixer(hn, att_w_qkv[j], att_q_norm_w[j], att_k_norm_w[j], att_w_o[j], cos, sin)
        h = h + mix
        h = h + conv_ffn(rms_norm(h, norm_ffn_w[i]), ffn_w_up[i], ffn_conv_w[i], ffn_conv_b[i], ffn_w_down[i])
    return h
```

```python
import functools
import math

import jax
import jax.numpy as jnp
from jax import lax
from jax.experimental import pallas as pl
from jax.experimental.pallas import tpu as pltpu

NORM_EPS = 1e-6
N_MIXERS = 2
SSM_GROUPS = 8
SSM_CHUNK = 128
ATT_KV_HEADS = 4
GRID_W = 64
ROPE_THETA = 10000.0

V7X_VMEM_BYTES = 64 * 1024 * 1024
V7X_LANES = 128
BF16_SUBLANES = 16

F32 = jnp.float32
BF16 = jnp.bfloat16


def _vmem_limit(block_bytes):
    return int(min(V7X_VMEM_BYTES - (6 << 20), max(32 << 20, 2 * block_bytes)))


def _params(semantics, block_bytes):
    return pltpu.CompilerParams(dimension_semantics=semantics, vmem_limit_bytes=_vmem_limit(block_bytes))


def _largest_tile(n, cap, quantum):
    best = n if n < quantum else None
    t = quantum
    while t <= min(n, cap):
        if n % t == 0:
            best = t
        t += quantum
    assert best is not None, (n, cap, quantum)
    return best


NORM_ROWS = 128


def _norm_matmul_kernel(h_ref, nw_ref, w_ref, o_ref, a_ref):
    @pl.when(pl.program_id(1) == 0)
    def _():
        def body(r, carry):
            rows = pl.ds(pl.multiple_of(r * NORM_ROWS, NORM_ROWS), NORM_ROWS)
            x = h_ref[rows, :]
            ms = jnp.mean(x * x, axis=-1, keepdims=True)
            a_ref[rows, :] = (x * lax.rsqrt(ms + NORM_EPS) * nw_ref[...]).astype(a_ref.dtype)
            return carry

        lax.fori_loop(0, h_ref.shape[0] // NORM_ROWS, body, 0)

    o_ref[...] = jnp.dot(a_ref[...], w_ref[...], preferred_element_type=F32).astype(o_ref.dtype)


def _norm_matmul(h, norm_w, w, out_dtype):
    m, k = h.shape
    n = w.shape[1]
    tm = _largest_tile(m, 1024, NORM_ROWS)
    tn = _largest_tile(n, 1024, V7X_LANES)
    ob = jnp.dtype(out_dtype).itemsize
    block_bytes = 2 * tm * k * 4 + 2 * k * tn * 2 + 2 * tm * tn * ob + tm * k * 2 + tm * tn * 4
    return pl.pallas_call(
        _norm_matmul_kernel,
        grid=(m // tm, n // tn),
        in_specs=[
            pl.BlockSpec((tm, k), lambda i, j: (i, 0)),
            pl.BlockSpec((1, k), lambda i, j: (0, 0)),
            pl.BlockSpec((k, tn), lambda i, j: (0, j)),
        ],
        out_specs=pl.BlockSpec((tm, tn), lambda i, j: (i, j)),
        out_shape=jax.ShapeDtypeStruct((m, n), out_dtype),
        scratch_shapes=[pltpu.VMEM((tm, k), BF16)],
        compiler_params=_params(("parallel", "arbitrary"), block_bytes),
        name="norm_matmul",
    )(h, norm_w.reshape(1, k).astype(F32), w)


def _matmul_residual_kernel(a_ref, w_ref, r_ref, o_ref):
    o_ref[...] = r_ref[...] + jnp.dot(a_ref[...], w_ref[...], preferred_element_type=F32)


def _matmul_residual(a, w, res):
    m, k = a.shape
    n = w.shape[1]
    tm = _largest_tile(m, 1024 if k <= 4096 else 512, V7X_LANES)
    tn = _largest_tile(n, 512, V7X_LANES)
    block_bytes = 2 * tm * k * 2 + 2 * k * tn * 2 + 4 * tm * tn * 4 + tm * tn * 4
    return pl.pallas_call(
        _matmul_residual_kernel,
        grid=(m // tm, n // tn),
        in_specs=[
            pl.BlockSpec((tm, k), lambda i, j: (i, 0)),
            pl.BlockSpec((k, tn), lambda i, j: (0, j)),
            pl.BlockSpec((tm, tn), lambda i, j: (i, j)),
        ],
        out_specs=pl.BlockSpec((tm, tn), lambda i, j: (i, j)),
        out_shape=jax.ShapeDtypeStruct((m, n), F32),
        compiler_params=_params(("parallel", "parallel"), block_bytes),
        name="matmul_residual",
    )(a, w, res)


CONV_ROWS = 256
CONV_HALO = BF16_SUBLANES


def _dwconv_chunk(x_ref, w, b, r0, seq_len):
    taps = w.shape[0]
    pad = taps // 2
    cur = x_ref[pl.ds(r0, CONV_ROWS), :].astype(F32)
    lo = pl.multiple_of(jnp.maximum(r0 - CONV_HALO, 0), CONV_HALO)
    hi = pl.multiple_of(jnp.minimum(r0 + CONV_ROWS, seq_len - CONV_HALO), CONV_HALO)
    prev = jnp.where(r0 > 0, x_ref[pl.ds(lo, CONV_HALO), :].astype(F32), 0.0)
    nxt = jnp.where(r0 + CONV_ROWS < seq_len, x_ref[pl.ds(hi, CONV_HALO), :].astype(F32), 0.0)
    ext = jnp.concatenate([prev, cur, nxt], axis=0)
    acc = cur * w[pad:pad + 1, :] + b
    for k in range(taps):
        if k != pad:
            start = CONV_HALO - pad + k
            acc = acc + ext[start:start + CONV_ROWS, :] * w[k:k + 1, :]
    return acc


def _silu(x):
    return x * (1.0 / (1.0 + jnp.exp(-x)))


def _conv_silu_kernel(x_ref, w_ref, b_ref, o_ref):
    seq_len = x_ref.shape[0]
    w = w_ref[...]
    b = b_ref[...]

    def body(c, carry):
        r0 = pl.multiple_of(c * CONV_ROWS, CONV_ROWS)
        o_ref[pl.ds(r0, CONV_ROWS), :] = _silu(_dwconv_chunk(x_ref, w, b, r0, seq_len)).astype(o_ref.dtype)
        return carry

    lax.fori_loop(0, seq_len // CONV_ROWS, body, 0)


def _conv_silu(x, col0, ncols, w, b):
    bsz, seq_len, _ = x.shape
    tc = _largest_tile(math.gcd(ncols, col0) if col0 else ncols, 512, V7X_LANES)
    taps = w.shape[0]
    block_bytes = 4 * seq_len * tc * 2 + 8 * CONV_ROWS * tc * 4
    return pl.pallas_call(
        _conv_silu_kernel,
        grid=(bsz, ncols // tc),
        in_specs=[
            pl.BlockSpec((None, seq_len, tc), lambda i, j: (i, 0, col0 // tc + j)),
            pl.BlockSpec((taps, tc), lambda i, j: (0, j)),
            pl.BlockSpec((1, tc), lambda i, j: (0, j)),
        ],
        out_specs=pl.BlockSpec((None, seq_len, tc), lambda i, j: (i, 0, j)),
        out_shape=jax.ShapeDtypeStruct((bsz, seq_len, ncols), BF16),
        compiler_params=_params(("parallel", "parallel"), block_bytes),
        name="conv_silu",
    )(x, w.astype(F32), b.reshape(1, ncols).astype(F32))


def _conv_glu_kernel(g_ref, v_ref, wg_ref, wv_ref, bg_ref, bv_ref, o_ref):
    seq_len = g_ref.shape[0]
    wg, wv, bg, bv = wg_ref[...], wv_ref[...], bg_ref[...], bv_ref[...]

    def body(c, carry):
        r0 = pl.multiple_of(c * CONV_ROWS, CONV_ROWS)
        gate = _dwconv_chunk(g_ref, wg, bg, r0, seq_len)
        val = _dwconv_chunk(v_ref, wv, bv, r0, seq_len)
        o_ref[pl.ds(r0, CONV_ROWS), :] = (_silu(gate) * val).astype(o_ref.dtype)
        return carry

    lax.fori_loop(0, seq_len // CONV_ROWS, body, 0)


def _conv_glu(u, w, b):
    bsz, seq_len, two_f = u.shape
    f = two_f // 2
    tc = _largest_tile(f, 512, V7X_LANES)
    nblk = f // tc
    taps = w.shape[0]
    block_bytes = 6 * seq_len * tc * 2 + 12 * CONV_ROWS * tc * 4
    w = w.astype(F32)
    b = b.reshape(1, two_f).astype(F32)
    return pl.pallas_call(
        _conv_glu_kernel,
        grid=(bsz, nblk),
        in_specs=[
            pl.BlockSpec((None, seq_len, tc), lambda i, j: (i, 0, j)),
            pl.BlockSpec((None, seq_len, tc), lambda i, j: (i, 0, nblk + j)),
            pl.BlockSpec((taps, tc), lambda i, j: (0, j)),
            pl.BlockSpec((taps, tc), lambda i, j: (0, nblk + j)),
            pl.BlockSpec((1, tc), lambda i, j: (0, j)),
            pl.BlockSpec((1, tc), lambda i, j: (0, nblk + j)),
        ],
        out_specs=pl.BlockSpec((None, seq_len, tc), lambda i, j: (i, 0, j)),
        out_shape=jax.ShapeDtypeStruct((bsz, seq_len, f), BF16),
        compiler_params=_params(("parallel", "parallel"), block_bytes),
        name="conv_glu",
    )(u, u, w, w, b, b)


def _split3(x):
    hi = x.astype(BF16)
    r1 = x - hi.astype(F32)
    mid = r1.astype(BF16)
    lo = (r1 - mid.astype(F32)).astype(BF16)
    return hi, mid, lo


def _softplus(x):
    return jnp.maximum(x, 0.0) + jnp.log(1.0 + jnp.exp(-jnp.abs(x)))


def _ssd_kernel(z_ref, x_ref, b_ref, c_ref, dtc_ref, dtr_ref, bias_c_ref, bias_r_ref, alog_c_ref, alog_r_ref,
                d_ref, nw_ref, y_ref, sf_ref, sb_ref, cdec_ref, carry_ref):
    t = SSM_CHUNK
    seq_len, gch = x_ref.shape
    nchunks = seq_len // t
    hpg = dtc_ref.shape[-1] // 2
    hd = gch // hpg
    npairs = gch // V7X_LANES
    assert hd * 2 == V7X_LANES and t == V7X_LANES

    row = lax.broadcasted_iota(jnp.int32, (t, t), 0)
    col = lax.broadcasted_iota(jnp.int32, (t, t), 1)
    lower = row >= col
    upper = col >= row
    tri = jnp.where(lower, 1.0, 0.0).astype(BF16)
    tri_t = jnp.where(upper, 1.0, 0.0).astype(BF16)
    tri3 = jnp.concatenate([tri, tri, tri], axis=1)
    tri_t3 = jnp.concatenate([tri_t, tri_t, tri_t], axis=0)
    e_row = lax.broadcasted_iota(jnp.int32, (6 * hpg, 2 * gch), 0)
    e_col = lax.broadcasted_iota(jnp.int32, (6 * hpg, 2 * gch), 1)
    assert hd & (hd - 1) == 0 and hpg & (hpg - 1) == 0
    expand3 = jnp.where((e_col >> (hd.bit_length() - 1)) == (e_row & (2 * hpg - 1)), 1.0, 0.0).astype(BF16)
    lane = lax.broadcasted_iota(jnp.int32, (t, V7X_LANES), 1)
    lo_half = lane < hd

    bias_c = bias_c_ref[...]
    a_c = -jnp.exp(alog_c_ref[...])
    bias_r = bias_r_ref[...]
    a_r = -jnp.exp(alog_r_ref[...])
    is_fwd_c = lax.broadcasted_iota(jnp.int32, (t, 2 * hpg), 1) < hpg

    def col_scalars(rows):
        dt = _softplus(dtc_ref[rows, :] + bias_c)
        a = dt * a_c
        cum = jnp.dot(tri3, jnp.concatenate(_split3(a), axis=0), preferred_element_type=F32)
        tot = cum[t - 1:t, :]
        return dt, a, cum, tot

    def local_states(c, carry):
        rows = pl.ds(pl.multiple_of(c * t, t), t)
        dt, a, cum, tot = col_scalars(rows)
        w = jnp.exp(jnp.where(is_fwd_c, tot - cum, cum - a)) * dt
        both = jnp.concatenate([w, jnp.broadcast_to(jnp.exp(tot), (8, 2 * hpg))], axis=0)
        wexp = jnp.dot(jnp.concatenate(_split3(both), axis=1), expand3, preferred_element_type=F32)
        cdec_ref[c] = wexp[t:, :]
        x = x_ref[rows, :].astype(F32)
        xd = jnp.concatenate([x * wexp[:t, :gch], x * wexp[:t, gch:]], axis=1).astype(BF16)
        loc = lax.dot_general(b_ref[rows, :], xd, (((0,), (0,)), ((), ())), preferred_element_type=F32)
        sf_ref[c] = loc[:, :gch].astype(BF16)
        sb_ref[c] = loc[:, gch:].astype(BF16)
        return carry

    lax.fori_loop(0, nchunks, local_states, 0)

    def scan(s_ref, lanes, reverse):
        carry_ref[...] = jnp.zeros_like(carry_ref)

        def body(i, carry):
            c = (nchunks - 1 - i) if reverse else i
            loc = s_ref[c].astype(F32)
            prev = carry_ref[...]
            s_ref[c] = prev.astype(BF16)
            carry_ref[...] = prev * cdec_ref[c, 0:1, lanes] + loc
            return carry

        lax.fori_loop(0, nchunks, body, 0)

    scan(sf_ref, slice(0, gch), False)
    scan(sb_ref, slice(gch, 2 * gch), True)

    dvec = d_ref[...]
    nw = nw_ref[...]

    def outputs(c, carry):
        rows = pl.ds(pl.multiple_of(c * t, t), t)
        lanes_t = pl.ds(pl.multiple_of(c * t, t), t)
        dt, a, cum, tot = col_scalars(rows)
        suf = tot - cum + a
        dtr = _softplus(dtr_ref[:, lanes_t] + bias_r)
        ar = dtr * a_r
        cumr = jnp.dot(jnp.concatenate(_split3(ar), axis=1), tri_t3, preferred_element_type=F32)
        sufr = cumr[:, t - 1:t] - cumr + ar
        xb = x_ref[rows, :]
        bm = b_ref[rows, :]
        cm = c_ref[rows, :]
        cb = lax.dot_general(cm, bm, (((1,), (1,)), ((), ())), preferred_element_type=F32)
        cmf = cm.astype(F32)
        sf = sf_ref[c]
        sb = sb_ref[c]
        ys = []
        for p in range(npairs):
            lhs, rhs = [], []
            sl = slice(p * V7X_LANES, (p + 1) * V7X_LANES)
            for q in range(2):
                j = 2 * p + q
                keep = lo_half if q == 0 else jnp.logical_not(lo_half)
                cf_col, cf_row = cum[:, j:j + 1], cumr[j:j + 1, :]
                sb_col, sb_row = suf[:, hpg + j:hpg + j + 1], sufr[hpg + j:hpg + j + 1, :]
                e = jnp.where(lower, cf_col - cf_row, sb_col - sb_row)
                wgt = jnp.where(lower, dtr[j:j + 1, :], 0.0) + jnp.where(upper, dtr[hpg + j:hpg + j + 1, :], 0.0)
                lhs += [cb * jnp.exp(e) * wgt, cmf * jnp.exp(cf_col), cmf * jnp.exp(sb_col)]
                zero = jnp.zeros((), BF16)
                rhs += [jnp.where(keep, xb[:, sl], zero), jnp.where(keep, sf[:, sl], zero),
                        jnp.where(keep, sb[:, sl], zero)]
            lhs = jnp.concatenate(lhs, axis=1).astype(BF16)
            rhs = jnp.concatenate(rhs, axis=0)
            ys.append(jnp.dot(lhs, rhs, preferred_element_type=F32))
        y = jnp.concatenate(ys, axis=1)
        y = y + xb.astype(F32) * dvec
        y = y * _silu(z_ref[rows, :].astype(F32))
        ms = jnp.mean(y * y, axis=-1, keepdims=True)
        y_ref[rows, :] = (y * lax.rsqrt(ms + NORM_EPS) * nw).astype(y_ref.dtype)
        return carry

    lax.fori_loop(0, nchunks, outputs, 0)


def _ssd(zx, xc, dt_raw, dt_bias, a_log, d_skip, norm_w, d_inner, heads):
    bsz, seq_len, _ = xc.shape
    g = SSM_GROUPS
    hpg = heads // g
    gch = d_inner // g
    state = (xc.shape[-1] - d_inner) // (2 * g)
    assert state == V7X_LANES
    nchunks = seq_len // SSM_CHUNK
    dt5 = dt_raw.reshape(bsz, seq_len, 2, g, hpg)
    dt_col = dt5.transpose(0, 3, 1, 2, 4).reshape(bsz, g, seq_len, 2 * hpg)
    dt_row = dt_col.transpose(0, 1, 3, 2)

    def per_group(p):
        return p.astype(F32).reshape(2, g, hpg).transpose(1, 0, 2).reshape(g, 2 * hpg)

    bias_g, alog_g = per_group(dt_bias), per_group(a_log)
    d_exp = jnp.repeat(d_skip.astype(F32), gch // hpg).reshape(g, 1, gch)
    nw = norm_w.astype(F32).reshape(g, 1, gch)
    xoff = d_inner // V7X_LANES
    block_bytes = (2 * 3 * seq_len * gch * 2 + 2 * 2 * seq_len * state * 2 + 2 * seq_len * V7X_LANES * 4
                   + 2 * 16 * seq_len * 4 + 2 * nchunks * state * gch * 2 + nchunks * 8 * 2 * gch * 4
                   + state * gch * 4)
    return pl.pallas_call(
        _ssd_kernel,
        grid=(bsz, g),
        in_specs=[
            pl.BlockSpec((None, seq_len, gch), lambda i, j: (i, 0, j)),
            pl.BlockSpec((None, seq_len, gch), lambda i, j: (i, 0, j)),
            pl.BlockSpec((None, seq_len, state), lambda i, j: (i, 0, xoff + j)),
            pl.BlockSpec((None, seq_len, state), lambda i, j: (i, 0, xoff + g + j)),
            pl.BlockSpec((None, None, seq_len, 2 * hpg), lambda i, j: (i, j, 0, 0)),
            pl.BlockSpec((None, None, 2 * hpg, seq_len), lambda i, j: (i, j, 0, 0)),
            pl.BlockSpec((None, 1, 2 * hpg), lambda i, j: (j, 0, 0)),
            pl.BlockSpec((None, 2 * hpg, 1), lambda i, j: (j, 0, 0)),
            pl.BlockSpec((None, 1, 2 * hpg), lambda i, j: (j, 0, 0)),
            pl.BlockSpec((None, 2 * hpg, 1), lambda i, j: (j, 0, 0)),
            pl.BlockSpec((None, 1, gch), lambda i, j: (j, 0, 0)),
            pl.BlockSpec((None, 1, gch), lambda i, j: (j, 0, 0)),
        ],
        out_specs=pl.BlockSpec((None, seq_len, gch), lambda i, j: (i, 0, j)),
        out_shape=jax.ShapeDtypeStruct((bsz, seq_len, d_inner), BF16),
        scratch_shapes=[
            pltpu.VMEM((nchunks, state, gch), BF16),
            pltpu.VMEM((nchunks, state, gch), BF16),
            pltpu.VMEM((nchunks, 8, 2 * gch), F32),
            pltpu.VMEM((state, gch), F32),
        ],
        compiler_params=_params(("parallel", "parallel"), block_bytes),
        name="ssd_scan",
    )(zx, xc, xc, xc, dt_col, dt_row, bias_g.reshape(g, 1, 2 * hpg), bias_g.reshape(g, 2 * hpg, 1),
      alog_g.reshape(g, 1, 2 * hpg), alog_g.reshape(g, 2 * hpg, 1), d_exp, nw)


QK_ROWS = 512


def _qk_prep_kernel(q_ref, k_ref, qw_ref, kw_ref, cos_ref, sin_ref, qo_ref, ko_ref, *, q_scale):
    dh = cos_ref.shape[-1]
    cos = cos_ref[...]
    sin = sin_ref[...]
    lane = lax.broadcasted_iota(jnp.int32, cos.shape, 1)
    assert dh & (dh - 1) == 0
    first = (lane & (dh // 2 - 1)) < (dh // 4)

    def prep(x, w, scale):
        x = x.astype(F32)
        ms = jnp.mean(x * x, axis=-1, keepdims=True)
        x = x * lax.rsqrt(ms + NORM_EPS) * w
        partner = jnp.where(first, pltpu.roll(x, dh - dh // 4, 1), pltpu.roll(x, dh // 4, 1))
        out = x * cos + partner * sin
        return out * scale if scale != 1.0 else out

    for h in range(q_ref.shape[-1] // dh):
        sl = slice(h * dh, (h + 1) * dh)
        qo_ref[:, sl] = prep(q_ref[:, sl], qw_ref[...], q_scale).astype(qo_ref.dtype)
    for h in range(k_ref.shape[-1] // dh):
        sl = slice(h * dh, (h + 1) * dh)
        ko_ref[:, sl] = prep(k_ref[:, sl], kw_ref[...], 1.0).astype(ko_ref.dtype)


def _qk_prep(qkv, q_norm_w, k_norm_w, cos_full, sin_signed, n_heads, q_scale):
    bsz, seq_len, _ = qkv.shape
    dh = q_norm_w.shape[0]
    qd, kd = n_heads * dh, ATT_KV_HEADS * dh
    assert qd % kd == 0
    tr = _largest_tile(seq_len, QK_ROWS, BF16_SUBLANES)
    block_bytes = 4 * tr * (qd + kd) * 2 + 16 * tr * dh * 4
    return pl.pallas_call(
        functools.partial(_qk_prep_kernel, q_scale=q_scale),
        grid=(bsz, seq_len // tr),
        in_specs=[
            pl.BlockSpec((None, tr, qd), lambda i, j: (i, j, 0)),
            pl.BlockSpec((None, tr, kd), lambda i, j: (i, j, qd // kd)),
            pl.BlockSpec((1, dh), lambda i, j: (0, 0)),
            pl.BlockSpec((1, dh), lambda i, j: (0, 0)),
            pl.BlockSpec((tr, dh), lambda i, j: (j, 0)),
            pl.BlockSpec((tr, dh), lambda i, j: (j, 0)),
        ],
        out_specs=[
            pl.BlockSpec((None, tr, qd), lambda i, j: (i, j, 0)),
            pl.BlockSpec((None, tr, kd), lambda i, j: (i, j, 0)),
        ],
        out_shape=[jax.ShapeDtypeStruct((bsz, seq_len, qd), BF16), jax.ShapeDtypeStruct((bsz, seq_len, kd), BF16)],
        compiler_params=_params(("parallel", "parallel"), block_bytes),
        name="qk_prep",
    )(qkv, qkv, q_norm_w.reshape(1, dh).astype(F32), k_norm_w.reshape(1, dh).astype(F32), cos_full, sin_signed)


ATT_Q_ROWS = 128


def _attention_kernel(q_ref, k_ref, v_ref, o_ref):
    tq, width = q_ref.shape
    dh = k_ref.shape[-1]
    rep = width // dh
    q = q_ref[...]
    qs = jnp.concatenate([q[:, r * dh:(r + 1) * dh] for r in range(rep)], axis=0)
    s = lax.dot_general(k_ref[...], qs, (((1,), (1,)), ((), ())), preferred_element_type=F32)
    m = jnp.max(s, axis=0, keepdims=True)
    p = jnp.exp2(s - m)
    denom = jnp.sum(p, axis=0, keepdims=True)
    acc = lax.dot_general(v_ref[...], p.astype(v_ref.dtype), (((0,), (0,)), ((), ())),
                          preferred_element_type=F32)
    out = (acc * (1.0 / denom)).T
    for r in range(rep):
        o_ref[:, r * dh:(r + 1) * dh] = out[r * tq:(r + 1) * tq, :].astype(o_ref.dtype)


def _attention(q, k, qkv, n_heads):
    bsz, seq_len, qd = q.shape
    dh = qd // n_heads
    rep = n_heads // ATT_KV_HEADS
    width = rep * dh
    voff = (n_heads + ATT_KV_HEADS)
    tq = ATT_Q_ROWS
    block_bytes = (4 * tq * width * 2 + 4 * seq_len * dh * 2 + seq_len * rep * tq * (4 + 4 + 2)
                   + 4 * rep * tq * dh * 4)
    return pl.pallas_call(
        _attention_kernel,
        grid=(bsz, ATT_KV_HEADS, seq_len // tq),
        in_specs=[
            pl.BlockSpec((None, tq, width), lambda b, h, i: (b, i, h)),
            pl.BlockSpec((None, seq_len, dh), lambda b, h, i: (b, 0, h)),
            pl.BlockSpec((None, seq_len, dh), lambda b, h, i: (b, 0, voff + h)),
        ],
        out_specs=pl.BlockSpec((None, tq, width), lambda b, h, i: (b, i, h)),
        out_shape=jax.ShapeDtypeStruct((bsz, seq_len, qd), BF16),
        compiler_params=_params(("parallel", "parallel", "parallel"), block_bytes),
        name="attention",
    )(q, k, qkv)


def _rope_tables(seq_len, dh):
    quarter = dh // 4
    tpos = jnp.arange(seq_len)
    pos = jnp.stack([tpos // GRID_W, tpos % GRID_W], axis=-1).astype(F32)
    inv_freq = ROPE_THETA ** (-jnp.arange(quarter, dtype=F32) / quarter)
    ang = pos[:, :, None] * inv_freq
    cos, sin = jnp.cos(ang), jnp.sin(ang)
    cos_full = jnp.concatenate([cos, cos], axis=-1).reshape(seq_len, dh)
    sin_signed = jnp.concatenate([-sin, sin], axis=-1).reshape(seq_len, dh)
    return cos_full, sin_signed


def _ssd_layer(h, norm_w, w_in, conv_w, conv_b, dt_bias, a_log, d_skip, gn_w, w_out):
    bsz, seq_len, d_model = h.shape
    d_inner = gn_w.shape[0]
    heads = a_log.shape[-1]
    conv_dim = conv_w.shape[-1]
    h2 = h.reshape(bsz * seq_len, d_model)
    w_main = w_in[:, :d_inner + conv_dim].astype(BF16)
    w_dt = w_in[:, d_inner + conv_dim:].astype(BF16)
    zx = _norm_matmul(h2, norm_w, w_main, BF16).reshape(bsz, seq_len, d_inner + conv_dim)
    dt_raw = _norm_matmul(h2, norm_w, w_dt, F32).reshape(bsz, seq_len, 2 * heads)
    xc = _conv_silu(zx, d_inner, conv_dim, conv_w, conv_b)
    y = _ssd(zx, xc, dt_raw, dt_bias, a_log, d_skip, gn_w, d_inner, heads)
    out = _matmul_residual(y.reshape(bsz * seq_len, d_inner), w_out.astype(BF16), h2)
    return out.reshape(bsz, seq_len, d_model)


def _gqa_layer(h, norm_w, w_qkv, q_norm_w, k_norm_w, w_o, tables):
    bsz, seq_len, d_model = h.shape
    dh = q_norm_w.shape[0]
    n_heads = w_o.shape[0] // dh
    h2 = h.reshape(bsz * seq_len, d_model)
    qkv = _norm_matmul(h2, norm_w, w_qkv.astype(BF16), BF16).reshape(bsz, seq_len, -1)
    q, k = _qk_prep(qkv, q_norm_w, k_norm_w, tables[0], tables[1], n_heads, math.log2(math.e) * dh ** -0.5)
    o = _attention(q, k, qkv, n_heads)
    out = _matmul_residual(o.reshape(bsz * seq_len, n_heads * dh), w_o.astype(BF16), h2)
    return out.reshape(bsz, seq_len, d_model)


def _ffn_layer(h, norm_w, w_up, conv_w, conv_b, w_down):
    bsz, seq_len, d_model = h.shape
    h2 = h.reshape(bsz * seq_len, d_model)
    u = _norm_matmul(h2, norm_w, w_up.astype(BF16), BF16).reshape(bsz, seq_len, -1)
    gated = _conv_glu(u, conv_w, conv_b)
    out = _matmul_residual(gated.reshape(bsz * seq_len, -1), w_down.astype(BF16), h2)
    return out.reshape(bsz, seq_len, d_model)


def kernel(x, norm_mix_w, norm_ffn_w, ssm_w_in, ssm_conv_w, ssm_conv_b, ssm_dt_bias, ssm_a_log, ssm_d, ssm_norm_w, ssm_w_out, att_w_qkv, att_q_norm_w, att_k_norm_w, att_w_o, ffn_w_up, ffn_conv_w, ffn_conv_b, ffn_w_down):
    depth = norm_mix_w.shape[0]
    tables = _rope_tables(x.shape[1], att_q_norm_w.shape[-1])
    h = x
    for i in range(depth):
        j = i // N_MIXERS
        if i % N_MIXERS == 0:
            h = _ssd_layer(h, norm_mix_w[i], ssm_w_in[j], ssm_conv_w[j], ssm_conv_b[j], ssm_dt_bias[j],
                           ssm_a_log[j], ssm_d[j], ssm_norm_w[j], ssm_w_out[j])
        else:
            h = _gqa_layer(h, norm_mix_w[i], att_w_qkv[j], att_q_norm_w[j], att_k_norm_w[j], att_w_o[j], tables)
        h = _ffn_layer(h, norm_ffn_w[i], ffn_w_up[i], ffn_conv_w[i], ffn_conv_b[i], ffn_w_down[i])
    return h
```

```python
import functools
import math

import jax
import jax.numpy as jnp
from jax import lax
from jax.experimental import pallas as pl
from jax.experimental.pallas import tpu as pltpu

NORM_EPS = 1e-6
N_MIXERS = 2
SSM_GROUPS = 8
SSM_CHUNK = 128
ATT_KV_HEADS = 4
GRID_W = 64
ROPE_THETA = 10000.0

V7X_VMEM_BYTES = 64 * 1024 * 1024
V7X_LANES = 128
BF16_SUBLANES = 16

F32 = jnp.float32
BF16 = jnp.bfloat16


def _vmem_limit(block_bytes):
    return int(min(V7X_VMEM_BYTES - (6 << 20), max(32 << 20, 2 * block_bytes)))


def _params(semantics, block_bytes):
    return pltpu.CompilerParams(dimension_semantics=semantics, vmem_limit_bytes=_vmem_limit(block_bytes))


def _largest_tile(n, cap, quantum):
    best = n if n < quantum else None
    t = quantum
    while t <= min(n, cap):
        if n % t == 0:
            best = t
        t += quantum
    assert best is not None, (n, cap, quantum)
    return best


NORM_ROWS = 128


def _norm_matmul_kernel(h_ref, nw_ref, w_ref, o_ref, a_ref):
    @pl.when(pl.program_id(1) == 0)
    def _():
        def body(r, carry):
            rows = pl.ds(pl.multiple_of(r * NORM_ROWS, NORM_ROWS), NORM_ROWS)
            x = h_ref[rows, :]
            ms = jnp.mean(x * x, axis=-1, keepdims=True)
            a_ref[rows, :] = (x * lax.rsqrt(ms + NORM_EPS) * nw_ref[...]).astype(a_ref.dtype)
            return carry

        lax.fori_loop(0, h_ref.shape[0] // NORM_ROWS, body, 0)

    o_ref[...] = jnp.dot(a_ref[...], w_ref[...], preferred_element_type=F32).astype(o_ref.dtype)


def _norm_matmul(h, norm_w, w, out_dtype):
    m, k = h.shape
    n = w.shape[1]
    tm = _largest_tile(m, 1024, NORM_ROWS)
    tn = _largest_tile(n, 1024, V7X_LANES)
    ob = jnp.dtype(out_dtype).itemsize
    block_bytes = 2 * tm * k * 4 + 2 * k * tn * 2 + 2 * tm * tn * ob + tm * k * 2 + tm * tn * 4
    return pl.pallas_call(
        _norm_matmul_kernel,
        grid=(m // tm, n // tn),
        in_specs=[
            pl.BlockSpec((tm, k), lambda i, j: (i, 0)),
            pl.BlockSpec((1, k), lambda i, j: (0, 0)),
            pl.BlockSpec((k, tn), lambda i, j: (0, j)),
        ],
        out_specs=pl.BlockSpec((tm, tn), lambda i, j: (i, j)),
        out_shape=jax.ShapeDtypeStruct((m, n), out_dtype),
        scratch_shapes=[pltpu.VMEM((tm, k), BF16)],
        compiler_params=_params(("parallel", "arbitrary"), block_bytes),
        name="norm_matmul",
    )(h, norm_w.reshape(1, k).astype(F32), w)


def _matmul_residual_kernel(a_ref, w_ref, r_ref, o_ref):
    o_ref[...] = r_ref[...] + jnp.dot(a_ref[...], w_ref[...], preferred_element_type=F32)


def _matmul_residual(a, w, res):
    m, k = a.shape
    n = w.shape[1]
    tm = _largest_tile(m, 1024 if k <= 4096 else 512, V7X_LANES)
    tn = _largest_tile(n, 512, V7X_LANES)
    block_bytes = 2 * tm * k * 2 + 2 * k * tn * 2 + 4 * tm * tn * 4 + tm * tn * 4
    return pl.pallas_call(
        _matmul_residual_kernel,
        grid=(m // tm, n // tn),
        in_specs=[
            pl.BlockSpec((tm, k), lambda i, j: (i, 0)),
            pl.BlockSpec((k, tn), lambda i, j: (0, j)),
            pl.BlockSpec((tm, tn), lambda i, j: (i, j)),
        ],
        out_specs=pl.BlockSpec((tm, tn), lambda i, j: (i, j)),
        out_shape=jax.ShapeDtypeStruct((m, n), F32),
        compiler_params=_params(("parallel", "parallel"), block_bytes),
        name="matmul_residual",
    )(a, w, res)


CONV_ROWS = 256
CONV_HALO = BF16_SUBLANES


def _dwconv_chunk(x_ref, w, b, r0, seq_len):
    taps = w.shape[0]
    pad = taps // 2
    cur = x_ref[pl.ds(r0, CONV_ROWS), :].astype(F32)
    lo = pl.multiple_of(jnp.maximum(r0 - CONV_HALO, 0), CONV_HALO)
    hi = pl.multiple_of(jnp.minimum(r0 + CONV_ROWS, seq_len - CONV_HALO), CONV_HALO)
    prev = jnp.where(r0 > 0, x_ref[pl.ds(lo, CONV_HALO), :].astype(F32), 0.0)
    nxt = jnp.where(r0 + CONV_ROWS < seq_len, x_ref[pl.ds(hi, CONV_HALO), :].astype(F32), 0.0)
    ext = jnp.concatenate([prev, cur, nxt], axis=0)
    acc = cur * w[pad:pad + 1, :] + b
    for k in range(taps):
        if k != pad:
            start = CONV_HALO - pad + k
            acc = acc + ext[start:start + CONV_ROWS, :] * w[k:k + 1, :]
    return acc


def _silu(x):
    return x * (1.0 / (1.0 + jnp.exp(-x)))


def _conv_silu_kernel(x_ref, w_ref, b_ref, o_ref):
    seq_len = x_ref.shape[0]
    w = w_ref[...]
    b = b_ref[...]

    def body(c, carry):
        r0 = pl.multiple_of(c * CONV_ROWS, CONV_ROWS)
        o_ref[pl.ds(r0, CONV_ROWS), :] = _silu(_dwconv_chunk(x_ref, w, b, r0, seq_len)).astype(o_ref.dtype)
        return carry

    lax.fori_loop(0, seq_len // CONV_ROWS, body, 0)


def _conv_silu(x, col0, ncols, w, b):
    bsz, seq_len, _ = x.shape
    tc = _largest_tile(math.gcd(ncols, col0) if col0 else ncols, 512, V7X_LANES)
    taps = w.shape[0]
    block_bytes = 4 * seq_len * tc * 2 + 8 * CONV_ROWS * tc * 4
    return pl.pallas_call(
        _conv_silu_kernel,
        grid=(bsz, ncols // tc),
        in_specs=[
            pl.BlockSpec((None, seq_len, tc), lambda i, j: (i, 0, col0 // tc + j)),
            pl.BlockSpec((taps, tc), lambda i, j: (0, j)),
            pl.BlockSpec((1, tc), lambda i, j: (0, j)),
        ],
        out_specs=pl.BlockSpec((None, seq_len, tc), lambda i, j: (i, 0, j)),
        out_shape=jax.ShapeDtypeStruct((bsz, seq_len, ncols), BF16),
        compiler_params=_params(("parallel", "parallel"), block_bytes),
        name="conv_silu",
    )(x, w.astype(F32), b.reshape(1, ncols).astype(F32))


def _conv_glu_kernel(g_ref, v_ref, wg_ref, wv_ref, bg_ref, bv_ref, o_ref):
    seq_len = g_ref.shape[0]
    wg, wv, bg, bv = wg_ref[...], wv_ref[...], bg_ref[...], bv_ref[...]

    def body(c, carry):
        r0 = pl.multiple_of(c * CONV_ROWS, CONV_ROWS)
        gate = _dwconv_chunk(g_ref, wg, bg, r0, seq_len)
        val = _dwconv_chunk(v_ref, wv, bv, r0, seq_len)
        o_ref[pl.ds(r0, CONV_ROWS), :] = (_silu(gate) * val).astype(o_ref.dtype)
        return carry

    lax.fori_loop(0, seq_len // CONV_ROWS, body, 0)


def _conv_glu(u, w, b):
    bsz, seq_len, two_f = u.shape
    f = two_f // 2
    tc = _largest_tile(f, 512, V7X_LANES)
    nblk = f // tc
    taps = w.shape[0]
    block_bytes = 6 * seq_len * tc * 2 + 12 * CONV_ROWS * tc * 4
    w = w.astype(F32)
    b = b.reshape(1, two_f).astype(F32)
    return pl.pallas_call(
        _conv_glu_kernel,
        grid=(bsz, nblk),
        in_specs=[
            pl.BlockSpec((None, seq_len, tc), lambda i, j: (i, 0, j)),
            pl.BlockSpec((None, seq_len, tc), lambda i, j: (i, 0, nblk + j)),
            pl.BlockSpec((taps, tc), lambda i, j: (0, j)),
            pl.BlockSpec((taps, tc), lambda i, j: (0, nblk + j)),
            pl.BlockSpec((1, tc), lambda i, j: (0, j)),
            pl.BlockSpec((1, tc), lambda i, j: (0, nblk + j)),
        ],
        out_specs=pl.BlockSpec((None, seq_len, tc), lambda i, j: (i, 0, j)),
        out_shape=jax.ShapeDtypeStruct((bsz, seq_len, f), BF16),
        compiler_params=_params(("parallel", "parallel"), block_bytes),
        name="conv_glu",
    )(u, u, w, w, b, b)


LOG2E = math.log2(math.e)


def _split3(x):
    hi = x.astype(BF16)
    r1 = x - hi.astype(F32)
    mid = r1.astype(BF16)
    lo = (r1 - mid.astype(F32)).astype(BF16)
    return hi, mid, lo


def _softplus(x):
    return jnp.maximum(x, 0.0) + jnp.log(1.0 + jnp.exp(-jnp.abs(x)))


def _ssd_kernel(z_ref, x_ref, b_ref, c_ref, dt_ref, bias_ref, alog_ref, d_ref, nw_ref, y_ref,
                s_ref, colval_ref, dts_ref, rowval_ref, dtrow_ref, tot_ref, cdec_ref, carry_ref, *, hpg):
    t = SSM_CHUNK
    seq_len, gch = x_ref.shape
    nchunks = seq_len // t
    period = 2 * hpg
    hd = gch // hpg
    npairs = gch // V7X_LANES
    assert hd * 2 == V7X_LANES and t == V7X_LANES and 3 * period <= V7X_LANES
    assert hd & (hd - 1) == 0 and period & (period - 1) == 0

    row = lax.broadcasted_iota(jnp.int32, (t, t), 0)
    col = lax.broadcasted_iota(jnp.int32, (t, t), 1)
    lower = row >= col
    upper = col >= row
    tri = jnp.where(lower, 1.0, 0.0).astype(BF16)
    lane = col
    is_fwd = (lane & (period - 1)) < hpg
    lo_half = lane < hd
    e_row = lax.broadcasted_iota(jnp.int32, (V7X_LANES, 2 * gch), 0)
    e_col = lax.broadcasted_iota(jnp.int32, (V7X_LANES, 2 * gch), 1)
    match = (e_row & (period - 1)) == (e_col >> (hd.bit_length() - 1))
    expand2 = jnp.where(match & (e_row < 2 * period), 1.0, 0.0).astype(BF16)
    expand3 = jnp.where(match & (e_row < 3 * period), 1.0, 0.0).astype(BF16)

    def expand_cols(v):
        hi = v.astype(BF16)
        mid = (v - hi.astype(F32)).astype(BF16)
        return jnp.dot(jnp.where(lane[:v.shape[0]] < period, hi, mid), expand2, preferred_element_type=F32)

    bias = bias_ref[...]
    a2c = -jnp.exp(alog_ref[...]) * LOG2E

    def scalars(c, carry):
        rows = pl.ds(pl.multiple_of(c * t, t), t)
        dt = _softplus(dt_ref[rows, :] + bias)
        a2 = dt * a2c
        cum3 = jnp.dot(tri, jnp.concatenate(_split3(a2), axis=1), preferred_element_type=F32)
        cum = cum3[:, :t] + cum3[:, t:2 * t] + cum3[:, 2 * t:]
        tot = cum[t - 1:t, :]
        colval = jnp.where(is_fwd, cum, tot - cum + a2)
        colval_ref[rows, :] = colval
        dts_ref[rows, :] = dt
        rowval_ref[:, rows] = colval.T[:period, :]
        dtrow_ref[:, rows] = dt.T[:period, :]
        tot_ref[pl.ds(c, 1), :] = tot
        return carry

    lax.fori_loop(0, nchunks, scalars, 0, unroll=4)

    cd3 = _split3(jnp.exp2(tot_ref[...]))
    lane_n = lax.broadcasted_iota(jnp.int32, (nchunks, V7X_LANES), 1)
    cd_lhs = jnp.where(lane_n < period, cd3[0], jnp.where(lane_n < 2 * period, cd3[1], cd3[2]))
    cdec_ref[...] = jnp.dot(cd_lhs, expand3, preferred_element_type=F32)

    def local_states(c, carry):
        rows = pl.ds(pl.multiple_of(c * t, t), t)
        colval = colval_ref[rows, :]
        tot = jnp.where(is_fwd[:1], colval[t - 1:t, :], colval[0:1, :])
        wexp = expand_cols(jnp.exp2(tot - colval) * dts_ref[rows, :])
        x = x_ref[rows, :].astype(F32)
        xd = jnp.concatenate([x * wexp[:, :gch], x * wexp[:, gch:]], axis=1).astype(BF16)
        loc = lax.dot_general(b_ref[rows, :], xd, (((0,), (0,)), ((), ())), preferred_element_type=F32)
        s_ref[c] = loc.astype(BF16)
        return carry

    lax.fori_loop(0, nchunks, local_states, 0, unroll=4)

    def scan(lanes, reverse):
        carry_ref[...] = jnp.zeros_like(carry_ref)

        def body(i, carry):
            c = (nchunks - 1 - i) if reverse else i
            loc = s_ref[c, :, lanes].astype(F32)
            prev = carry_ref[...]
            s_ref[c, :, lanes] = prev.astype(BF16)
            carry_ref[...] = prev * cdec_ref[pl.ds(c, 1), lanes] + loc
            return carry

        lax.fori_loop(0, nchunks, body, 0, unroll=2)

    scan(slice(0, gch), False)
    scan(slice(gch, 2 * gch), True)

    dvec = d_ref[...]
    nw = nw_ref[...]
    zero = jnp.zeros((), BF16)

    def outputs(c, carry):
        rows = pl.ds(pl.multiple_of(c * t, t), t)
        colval = colval_ref[rows, :]
        rowval = rowval_ref[:, rows]
        dtr = dtrow_ref[:, rows]
        xb = x_ref[rows, :]
        bm = b_ref[rows, :]
        cm = c_ref[rows, :]
        cs = jnp.dot(cm, s_ref[c], preferred_element_type=F32) * expand_cols(jnp.exp2(colval))
        y = cs[:, :gch] + cs[:, gch:] + xb.astype(F32) * dvec
        cb = lax.dot_general(cm, bm, (((1,), (1,)), ((), ())), preferred_element_type=F32)
        ys = []
        for p in range(npairs):
            ms = []
            for q in range(2):
                j = 2 * p + q
                e = jnp.where(lower, colval[:, j:j + 1] - rowval[j:j + 1, :],
                              colval[:, hpg + j:hpg + j + 1] - rowval[hpg + j:hpg + j + 1, :])
                wgt = jnp.where(lower, dtr[j:j + 1, :], 0.0) + jnp.where(upper, dtr[hpg + j:hpg + j + 1, :], 0.0)
                ms.append(cb * jnp.exp2(e) * wgt)
            xp = xb[:, p * V7X_LANES:(p + 1) * V7X_LANES]
            rhs = jnp.concatenate([jnp.where(lo_half, xp, zero), jnp.where(lo_half, zero, xp)], axis=0)
            ys.append(jnp.dot(jnp.concatenate(ms, axis=1).astype(BF16), rhs, preferred_element_type=F32))
        y = y + jnp.concatenate(ys, axis=1)
        y = y * _silu(z_ref[rows, :].astype(F32))
        ms = jnp.mean(y * y, axis=-1, keepdims=True)
        y_ref[rows, :] = (y * lax.rsqrt(ms + NORM_EPS) * nw).astype(y_ref.dtype)
        return carry

    lax.fori_loop(0, nchunks, outputs, 0, unroll=2)


def _ssd(zx, xc, dt_raw, dt_bias, a_log, d_skip, norm_w, d_inner, heads):
    bsz, seq_len, _ = xc.shape
    g = SSM_GROUPS
    hpg = heads // g
    gch = d_inner // g
    state = (xc.shape[-1] - d_inner) // (2 * g)
    assert state == V7X_LANES
    nchunks = seq_len // SSM_CHUNK
    reps = V7X_LANES // (2 * hpg)

    def per_group(p, lead):
        p = p.astype(F32).reshape(lead + (2, g, hpg))
        p = jnp.moveaxis(p, -3, -2).reshape(lead + (g, 2 * hpg))
        return jnp.tile(p, (1,) * (len(lead) + 1) + (reps,))

    dt_t = jnp.moveaxis(per_group(dt_raw, (bsz, seq_len)), 2, 1)
    bias_g = per_group(dt_bias, ()).reshape(g, 1, V7X_LANES)
    alog_g = per_group(a_log, ()).reshape(g, 1, V7X_LANES)
    d_exp = jnp.repeat(d_skip.astype(F32), gch // hpg).reshape(g, 1, gch)
    nw = norm_w.astype(F32).reshape(g, 1, gch)
    xoff = d_inner // V7X_LANES
    block_bytes = (2 * 3 * seq_len * gch * 2 + 2 * 2 * seq_len * state * 2 + 2 * seq_len * V7X_LANES * 4
                   + nchunks * state * 2 * gch * 2 + 2 * seq_len * V7X_LANES * 4 + 4 * 2 * hpg * seq_len * 4
                   + nchunks * 2 * gch * 4 + state * gch * 4)
    return pl.pallas_call(
        functools.partial(_ssd_kernel, hpg=hpg),
        grid=(bsz, g),
        in_specs=[
            pl.BlockSpec((None, seq_len, gch), lambda i, j: (i, 0, j)),
            pl.BlockSpec((None, seq_len, gch), lambda i, j: (i, 0, j)),
            pl.BlockSpec((None, seq_len, state), lambda i, j: (i, 0, xoff + j)),
            pl.BlockSpec((None, seq_len, state), lambda i, j: (i, 0, xoff + g + j)),
            pl.BlockSpec((None, None, seq_len, V7X_LANES), lambda i, j: (i, j, 0, 0)),
            pl.BlockSpec((None, 1, V7X_LANES), lambda i, j: (j, 0, 0)),
            pl.BlockSpec((None, 1, V7X_LANES), lambda i, j: (j, 0, 0)),
            pl.BlockSpec((None, 1, gch), lambda i, j: (j, 0, 0)),
            pl.BlockSpec((None, 1, gch), lambda i, j: (j, 0, 0)),
        ],
        out_specs=pl.BlockSpec((None, seq_len, gch), lambda i, j: (i, 0, j)),
        out_shape=jax.ShapeDtypeStruct((bsz, seq_len, d_inner), BF16),
        scratch_shapes=[
            pltpu.VMEM((nchunks, state, 2 * gch), BF16),
            pltpu.VMEM((seq_len, V7X_LANES), F32),
            pltpu.VMEM((seq_len, V7X_LANES), F32),
            pltpu.VMEM((2 * hpg, seq_len), F32),
            pltpu.VMEM((2 * hpg, seq_len), F32),
            pltpu.VMEM((nchunks, V7X_LANES), F32),
            pltpu.VMEM((nchunks, 2 * gch), F32),
            pltpu.VMEM((state, gch), F32),
        ],
        compiler_params=_params(("parallel", "parallel"), block_bytes),
        name="ssd_scan",
    )(zx, xc, xc, xc, dt_t, bias_g, alog_g, d_exp, nw)


QK_ROWS = 512


def _qk_prep_kernel(q_ref, k_ref, qw_ref, kw_ref, cos_ref, sin_ref, qo_ref, ko_ref, *, q_scale):
    dh = cos_ref.shape[-1]
    cos = cos_ref[...]
    sin = sin_ref[...]
    lane = lax.broadcasted_iota(jnp.int32, cos.shape, 1)
    assert dh & (dh - 1) == 0
    first = (lane & (dh // 2 - 1)) < (dh // 4)

    def prep(x, w, scale):
        x = x.astype(F32)
        ms = jnp.mean(x * x, axis=-1, keepdims=True)
        x = x * lax.rsqrt(ms + NORM_EPS) * w
        partner = jnp.where(first, pltpu.roll(x, dh - dh // 4, 1), pltpu.roll(x, dh // 4, 1))
        out = x * cos + partner * sin
        return out * scale if scale != 1.0 else out

    for h in range(q_ref.shape[-1] // dh):
        sl = slice(h * dh, (h + 1) * dh)
        qo_ref[:, sl] = prep(q_ref[:, sl], qw_ref[...], q_scale).astype(qo_ref.dtype)
    for h in range(k_ref.shape[-1] // dh):
        sl = slice(h * dh, (h + 1) * dh)
        ko_ref[:, sl] = prep(k_ref[:, sl], kw_ref[...], 1.0).astype(ko_ref.dtype)


def _qk_prep(qkv, q_norm_w, k_norm_w, cos_full, sin_signed, n_heads, q_scale):
    bsz, seq_len, _ = qkv.shape
    dh = q_norm_w.shape[0]
    qd, kd = n_heads * dh, ATT_KV_HEADS * dh
    assert qd % kd == 0
    tr = _largest_tile(seq_len, QK_ROWS, BF16_SUBLANES)
    block_bytes = 4 * tr * (qd + kd) * 2 + 16 * tr * dh * 4
    return pl.pallas_call(
        functools.partial(_qk_prep_kernel, q_scale=q_scale),
        grid=(bsz, seq_len // tr),
        in_specs=[
            pl.BlockSpec((None, tr, qd), lambda i, j: (i, j, 0)),
            pl.BlockSpec((None, tr, kd), lambda i, j: (i, j, qd // kd)),
            pl.BlockSpec((1, dh), lambda i, j: (0, 0)),
            pl.BlockSpec((1, dh), lambda i, j: (0, 0)),
            pl.BlockSpec((tr, dh), lambda i, j: (j, 0)),
            pl.BlockSpec((tr, dh), lambda i, j: (j, 0)),
        ],
        out_specs=[
            pl.BlockSpec((None, tr, qd), lambda i, j: (i, j, 0)),
            pl.BlockSpec((None, tr, kd), lambda i, j: (i, j, 0)),
        ],
        out_shape=[jax.ShapeDtypeStruct((bsz, seq_len, qd), BF16), jax.ShapeDtypeStruct((bsz, seq_len, kd), BF16)],
        compiler_params=_params(("parallel", "parallel"), block_bytes),
        name="qk_prep",
    )(qkv, qkv, q_norm_w.reshape(1, dh).astype(F32), k_norm_w.reshape(1, dh).astype(F32), cos_full, sin_signed)


ATT_Q_ROWS = 128


def _attention_kernel(q_ref, k_ref, v_ref, o_ref):
    tq, width = q_ref.shape
    dh = k_ref.shape[-1]
    rep = width // dh
    q = q_ref[...]
    qs = jnp.concatenate([q[:, r * dh:(r + 1) * dh] for r in range(rep)], axis=0)
    s = lax.dot_general(k_ref[...], qs, (((1,), (1,)), ((), ())), preferred_element_type=F32)
    m = jnp.max(s, axis=0, keepdims=True)
    p = jnp.exp2(s - m)
    denom = jnp.sum(p, axis=0, keepdims=True)
    acc = lax.dot_general(v_ref[...], p.astype(v_ref.dtype), (((0,), (0,)), ((), ())),
                          preferred_element_type=F32)
    out = (acc * (1.0 / denom)).T
    for r in range(rep):
        o_ref[:, r * dh:(r + 1) * dh] = out[r * tq:(r + 1) * tq, :].astype(o_ref.dtype)


def _attention(q, k, qkv, n_heads):
    bsz, seq_len, qd = q.shape
    dh = qd // n_heads
    rep = n_heads // ATT_KV_HEADS
    width = rep * dh
    voff = (n_heads + ATT_KV_HEADS)
    tq = ATT_Q_ROWS
    block_bytes = (4 * tq * width * 2 + 4 * seq_len * dh * 2 + seq_len * rep * tq * (4 + 4 + 2)
                   + 4 * rep * tq * dh * 4)
    return pl.pallas_call(
        _attention_kernel,
        grid=(bsz, ATT_KV_HEADS, seq_len // tq),
        in_specs=[
            pl.BlockSpec((None, tq, width), lambda b, h, i: (b, i, h)),
            pl.BlockSpec((None, seq_len, dh), lambda b, h, i: (b, 0, h)),
            pl.BlockSpec((None, seq_len, dh), lambda b, h, i: (b, 0, voff + h)),
        ],
        out_specs=pl.BlockSpec((None, tq, width), lambda b, h, i: (b, i, h)),
        out_shape=jax.ShapeDtypeStruct((bsz, seq_len, qd), BF16),
        compiler_params=_params(("parallel", "parallel", "parallel"), block_bytes),
        name="attention",
    )(q, k, qkv)


def _rope_tables(seq_len, dh):
    quarter = dh // 4
    tpos = jnp.arange(seq_len)
    pos = jnp.stack([tpos // GRID_W, tpos % GRID_W], axis=-1).astype(F32)
    inv_freq = ROPE_THETA ** (-jnp.arange(quarter, dtype=F32) / quarter)
    ang = pos[:, :, None] * inv_freq
    cos, sin = jnp.cos(ang), jnp.sin(ang)
    cos_full = jnp.concatenate([cos, cos], axis=-1).reshape(seq_len, dh)
    sin_signed = jnp.concatenate([-sin, sin], axis=-1).reshape(seq_len, dh)
    return cos_full, sin_signed


def _ssd_layer(h, norm_w, w_in, conv_w, conv_b, dt_bias, a_log, d_skip, gn_w, w_out):
    bsz, seq_len, d_model = h.shape
    d_inner = gn_w.shape[0]
    heads = a_log.shape[-1]
    conv_dim = conv_w.shape[-1]
    h2 = h.reshape(bsz * seq_len, d_model)
    w_main = w_in[:, :d_inner + conv_dim].astype(BF16)
    w_dt = w_in[:, d_inner + conv_dim:].astype(BF16)
    zx = _norm_matmul(h2, norm_w, w_main, BF16).reshape(bsz, seq_len, d_inner + conv_dim)
    dt_raw = _norm_matmul(h2, norm_w, w_dt, F32).reshape(bsz, seq_len, 2 * heads)
    xc = _conv_silu(zx, d_inner, conv_dim, conv_w, conv_b)
    y = _ssd(zx, xc, dt_raw, dt_bias, a_log, d_skip, gn_w, d_inner, heads)
    out = _matmul_residual(y.reshape(bsz * seq_len, d_inner), w_out.astype(BF16), h2)
    return out.reshape(bsz, seq_len, d_model)


def _gqa_layer(h, norm_w, w_qkv, q_norm_w, k_norm_w, w_o, tables):
    bsz, seq_len, d_model = h.shape
    dh = q_norm_w.shape[0]
    n_heads = w_o.shape[0] // dh
    h2 = h.reshape(bsz * seq_len, d_model)
    qkv = _norm_matmul(h2, norm_w, w_qkv.astype(BF16), BF16).reshape(bsz, seq_len, -1)
    q, k = _qk_prep(qkv, q_norm_w, k_norm_w, tables[0], tables[1], n_heads, LOG2E * dh ** -0.5)
    o = _attention(q, k, qkv, n_heads)
    out = _matmul_residual(o.reshape(bsz * seq_len, n_heads * dh), w_o.astype(BF16), h2)
    return out.reshape(bsz, seq_len, d_model)


def _ffn_layer(h, norm_w, w_up, conv_w, conv_b, w_down):
    bsz, seq_len, d_model = h.shape
    h2 = h.reshape(bsz * seq_len, d_model)
    u = _norm_matmul(h2, norm_w, w_up.astype(BF16), BF16).reshape(bsz, seq_len, -1)
    gated = _conv_glu(u, conv_w, conv_b)
    out = _matmul_residual(gated.reshape(bsz * seq_len, -1), w_down.astype(BF16), h2)
    return out.reshape(bsz, seq_len, d_model)


def kernel(x, norm_mix_w, norm_ffn_w, ssm_w_in, ssm_conv_w, ssm_conv_b, ssm_dt_bias, ssm_a_log, ssm_d, ssm_norm_w, ssm_w_out, att_w_qkv, att_q_norm_w, att_k_norm_w, att_w_o, ffn_w_up, ffn_conv_w, ffn_conv_b, ffn_w_down):
    depth = norm_mix_w.shape[0]
    tables = _rope_tables(x.shape[1], att_q_norm_w.shape[-1])
    h = x
    for i in range(depth):
        j = i // N_MIXERS
        if i % N_MIXERS == 0:
            h = _ssd_layer(h, norm_mix_w[i], ssm_w_in[j], ssm_conv_w[j], ssm_conv_b[j], ssm_dt_bias[j],
                           ssm_a_log[j], ssm_d[j], ssm_norm_w[j], ssm_w_out[j])
        else:
            h = _gqa_layer(h, norm_mix_w[i], att_w_qkv[j], att_q_norm_w[j], att_k_norm_w[j], att_w_o[j], tables)
        h = _ffn_layer(h, norm_ffn_w[i], ffn_w_up[i], ffn_conv_w[i], ffn_conv_b[i], ffn_w_down[i])
    return h
```

```python
import functools
import math

import jax
import jax.numpy as jnp
from jax import lax
from jax.experimental import pallas as pl
from jax.experimental.pallas import tpu as pltpu

NORM_EPS = 1e-6
N_MIXERS = 2
SSM_GROUPS = 8
SSM_CHUNK = 128
ATT_KV_HEADS = 4
GRID_W = 64
ROPE_THETA = 10000.0

V7X_VMEM_BYTES = 64 * 1024 * 1024
V7X_LANES = 128
BF16_SUBLANES = 16

F32 = jnp.float32
BF16 = jnp.bfloat16


def _vmem_limit(block_bytes):
    return int(min(V7X_VMEM_BYTES - (6 << 20), max(32 << 20, 2 * block_bytes)))


def _params(semantics, block_bytes):
    return pltpu.CompilerParams(dimension_semantics=semantics, vmem_limit_bytes=_vmem_limit(block_bytes))


def _largest_tile(n, cap, quantum):
    best = n if n < quantum else None
    t = quantum
    while t <= min(n, cap):
        if n % t == 0:
            best = t
        t += quantum
    assert best is not None, (n, cap, quantum)
    return best


NORM_ROWS = 128


def _norm_matmul_kernel(h_ref, nw_ref, w_ref, o_ref, a_ref):
    @pl.when(pl.program_id(1) == 0)
    def _():
        def body(r, carry):
            rows = pl.ds(pl.multiple_of(r * NORM_ROWS, NORM_ROWS), NORM_ROWS)
            x = h_ref[rows, :]
            ms = jnp.mean(x * x, axis=-1, keepdims=True)
            a_ref[rows, :] = (x * lax.rsqrt(ms + NORM_EPS) * nw_ref[...]).astype(a_ref.dtype)
            return carry

        lax.fori_loop(0, h_ref.shape[0] // NORM_ROWS, body, 0)

    o_ref[...] = jnp.dot(a_ref[...], w_ref[...], preferred_element_type=F32).astype(o_ref.dtype)


def _norm_matmul(h, norm_w, w, out_dtype):
    m, k = h.shape
    n = w.shape[1]
    tm = _largest_tile(m, 1024, NORM_ROWS)
    tn = _largest_tile(n, 1024, V7X_LANES)
    ob = jnp.dtype(out_dtype).itemsize
    block_bytes = 2 * tm * k * 4 + 2 * k * tn * 2 + 2 * tm * tn * ob + tm * k * 2 + tm * tn * 4
    return pl.pallas_call(
        _norm_matmul_kernel,
        grid=(m // tm, n // tn),
        in_specs=[
            pl.BlockSpec((tm, k), lambda i, j: (i, 0)),
            pl.BlockSpec((1, k), lambda i, j: (0, 0)),
            pl.BlockSpec((k, tn), lambda i, j: (0, j)),
        ],
        out_specs=pl.BlockSpec((tm, tn), lambda i, j: (i, j)),
        out_shape=jax.ShapeDtypeStruct((m, n), out_dtype),
        scratch_shapes=[pltpu.VMEM((tm, k), BF16)],
        compiler_params=_params(("parallel", "arbitrary"), block_bytes),
        name="norm_matmul",
    )(h, norm_w.reshape(1, k).astype(F32), w)


def _matmul_residual_kernel(a_ref, w_ref, r_ref, o_ref):
    o_ref[...] = r_ref[...] + jnp.dot(a_ref[...], w_ref[...], preferred_element_type=F32)


def _matmul_residual(a, w, res):
    m, k = a.shape
    n = w.shape[1]
    tm = _largest_tile(m, 1024 if k <= 4096 else 512, V7X_LANES)
    tn = _largest_tile(n, 512, V7X_LANES)
    block_bytes = 2 * tm * k * 2 + 2 * k * tn * 2 + 4 * tm * tn * 4 + tm * tn * 4
    return pl.pallas_call(
        _matmul_residual_kernel,
        grid=(m // tm, n // tn),
        in_specs=[
            pl.BlockSpec((tm, k), lambda i, j: (i, 0)),
            pl.BlockSpec((k, tn), lambda i, j: (0, j)),
            pl.BlockSpec((tm, tn), lambda i, j: (i, j)),
        ],
        out_specs=pl.BlockSpec((tm, tn), lambda i, j: (i, j)),
        out_shape=jax.ShapeDtypeStruct((m, n), F32),
        compiler_params=_params(("parallel", "parallel"), block_bytes),
        name="matmul_residual",
    )(a, w, res)


CONV_ROWS = 256
CONV_HALO = BF16_SUBLANES
CONV_SUB_COLS = 256


def _silu(x):
    half = 0.5 * x
    return half + half * jnp.tanh(half)


def _rmsnorm_rows(x, nw):
    ms = jnp.mean(x * x, axis=-1, keepdims=True)
    return (x * lax.rsqrt(ms + NORM_EPS) * nw).astype(BF16)


def _norm_matmul_conv_kernel(*refs, branches, tiles_per_seq):
    h_ref, hp_ref, hn_ref, nw_ref = refs[:4]
    w_refs = refs[4:4 + branches]
    cw_refs = refs[4 + branches:4 + 2 * branches]
    cb_refs = refs[4 + 2 * branches:4 + 3 * branches]
    o_ref, a_ref = refs[4 + 3 * branches], refs[5 + 3 * branches]
    u_refs = refs[6 + 3 * branches:]
    i, j = pl.program_id(0), pl.program_id(1)
    tm = h_ref.shape[0]
    halo = CONV_HALO

    @pl.when(j == 0)
    def _():
        nw = nw_ref[...]
        starts_seq = (i % tiles_per_seq) == 0
        ends_seq = (i % tiles_per_seq) == tiles_per_seq - 1
        a_ref[0:halo, :] = jnp.where(starts_seq, 0.0, _rmsnorm_rows(hp_ref[...], nw).astype(F32)).astype(BF16)
        a_ref[halo + tm:, :] = jnp.where(ends_seq, 0.0, _rmsnorm_rows(hn_ref[...], nw).astype(F32)).astype(BF16)

        def body(r, carry):
            rows = pl.multiple_of(r * NORM_ROWS, NORM_ROWS)
            a_ref[pl.ds(halo + rows, NORM_ROWS), :] = _rmsnorm_rows(h_ref[pl.ds(rows, NORM_ROWS), :], nw)
            return carry

        lax.fori_loop(0, tm // NORM_ROWS, body, 0)

    taps = cw_refs[0].shape[0]
    pad = taps // 2
    tn = o_ref.shape[1]
    sub = u_refs[0].shape[-1]
    for s, n0 in enumerate(range(0, tn, sub)):
        cols = slice(n0, n0 + sub)
        slot = s % 2
        for w_ref, u_ref in zip(w_refs, u_refs):
            u_ref[slot] = jnp.dot(a_ref[...], w_ref[:, cols], preferred_element_type=F32)
        for r0 in range(0, tm, CONV_ROWS):
            convs = []
            for u_ref, cw_ref, cb_ref in zip(u_refs, cw_refs, cb_refs):
                acc = cb_ref[:, cols]
                for k in range(taps):
                    start = halo + r0 - pad + k
                    acc = acc + u_ref[slot, start:start + CONV_ROWS, :] * cw_ref[k:k + 1, cols]
                convs.append(acc)
            out = _silu(convs[0]) * convs[1] if branches == 2 else _silu(convs[0])
            o_ref[r0:r0 + CONV_ROWS, cols] = out.astype(o_ref.dtype)


def _norm_matmul_conv(h, norm_w, ws, conv_ws, conv_bs, seq_len, sub_cols=None):
    m, k = h.shape
    n = ws[0].shape[1]
    branches = len(ws)
    tm = _largest_tile(seq_len, 1024, CONV_ROWS)
    tn = _largest_tile(n, 1024 // branches, V7X_LANES)
    sub = tn if sub_cols is None else _largest_tile(tn, sub_cols, V7X_LANES)
    hb = tm // CONV_HALO
    last_hb = m // CONV_HALO - 1
    taps = conv_ws[0].shape[0]
    ext = tm + 2 * CONV_HALO
    block_bytes = (2 * ext * k * 4 + branches * 2 * k * tn * 2 + 2 * tm * tn * 2 + ext * k * 2
                   + 3 * branches * ext * sub * 4 + 8 * CONV_ROWS * tn * 4)
    w_spec = pl.BlockSpec((k, tn), lambda i, j: (0, j))
    cw_spec = pl.BlockSpec((taps, tn), lambda i, j: (0, j))
    cb_spec = pl.BlockSpec((1, tn), lambda i, j: (0, j))
    return pl.pallas_call(
        functools.partial(_norm_matmul_conv_kernel, branches=branches, tiles_per_seq=seq_len // tm),
        grid=(m // tm, n // tn),
        in_specs=[
            pl.BlockSpec((tm, k), lambda i, j: (i, 0)),
            pl.BlockSpec((CONV_HALO, k), lambda i, j: (jnp.maximum(i * hb - 1, 0), 0)),
            pl.BlockSpec((CONV_HALO, k), lambda i, j: (jnp.minimum((i + 1) * hb, last_hb), 0)),
            pl.BlockSpec((1, k), lambda i, j: (0, 0)),
        ] + [w_spec] * branches + [cw_spec] * branches + [cb_spec] * branches,
        out_specs=pl.BlockSpec((tm, tn), lambda i, j: (i, j)),
        out_shape=jax.ShapeDtypeStruct((m, n), BF16),
        scratch_shapes=[pltpu.VMEM((ext, k), BF16)] + [pltpu.VMEM((2, ext, sub), F32)] * branches,
        compiler_params=_params(("parallel", "arbitrary"), block_bytes),
        name="norm_matmul_conv",
    )(h, h, h, norm_w.reshape(1, k).astype(F32), *ws, *[w.astype(F32) for w in conv_ws],
      *[b.reshape(1, n).astype(F32) for b in conv_bs])


LOG2E = math.log2(math.e)


def _split3(x):
    hi = x.astype(BF16)
    r1 = x - hi.astype(F32)
    mid = r1.astype(BF16)
    lo = (r1 - mid.astype(F32)).astype(BF16)
    return hi, mid, lo


def _softplus(x):
    return jnp.maximum(x, 0.0) + jnp.log(1.0 + jnp.exp(-jnp.abs(x)))


def _ssd_kernel(z_ref, x_ref, b_ref, c_ref, dt_ref, bias_ref, alog_ref, d_ref, nw_ref, y_ref,
                s_ref, colval_ref, dts_ref, rowval_ref, dtrow_ref, tot_ref, cdec_ref, carry_ref, *, hpg):
    t = SSM_CHUNK
    seq_len, gch = x_ref.shape
    nchunks = seq_len // t
    period = 2 * hpg
    hd = gch // hpg
    npairs = gch // V7X_LANES
    assert hd * 2 == V7X_LANES and t == V7X_LANES and 3 * period <= V7X_LANES
    assert hd & (hd - 1) == 0 and period & (period - 1) == 0

    row = lax.broadcasted_iota(jnp.int32, (t, t), 0)
    col = lax.broadcasted_iota(jnp.int32, (t, t), 1)
    lower = row >= col
    upper = col >= row
    tri = jnp.where(lower, 1.0, 0.0).astype(BF16)
    lane = col
    is_fwd = (lane & (period - 1)) < hpg
    lo_half = lane < hd
    e_row = lax.broadcasted_iota(jnp.int32, (V7X_LANES, 2 * gch), 0)
    e_col = lax.broadcasted_iota(jnp.int32, (V7X_LANES, 2 * gch), 1)
    match = (e_row & (period - 1)) == (e_col >> (hd.bit_length() - 1))
    expand2 = jnp.where(match & (e_row < 2 * period), 1.0, 0.0).astype(BF16)
    expand3 = jnp.where(match & (e_row < 3 * period), 1.0, 0.0).astype(BF16)

    def expand_cols(v):
        hi = v.astype(BF16)
        mid = (v - hi.astype(F32)).astype(BF16)
        return jnp.dot(jnp.where(lane[:v.shape[0]] < period, hi, mid), expand2, preferred_element_type=F32)

    bias = bias_ref[...]
    a2c = -jnp.exp(alog_ref[...]) * LOG2E

    def scalars(c, carry):
        rows = pl.ds(pl.multiple_of(c * t, t), t)
        dt = _softplus(dt_ref[rows, :] + bias)
        a2 = dt * a2c
        cum3 = jnp.dot(tri, jnp.concatenate(_split3(a2), axis=1), preferred_element_type=F32)
        cum = cum3[:, :t] + cum3[:, t:2 * t] + cum3[:, 2 * t:]
        tot = cum[t - 1:t, :]
        colval = jnp.where(is_fwd, cum, tot - cum + a2)
        colval_ref[rows, :] = colval
        dts_ref[rows, :] = dt
        rowval_ref[:, rows] = colval.T[:period, :]
        dtrow_ref[:, rows] = dt.T[:period, :]
        tot_ref[pl.ds(c, 1), :] = tot
        return carry

    lax.fori_loop(0, nchunks, scalars, 0, unroll=4)

    cd3 = _split3(jnp.exp2(tot_ref[...]))
    lane_n = lax.broadcasted_iota(jnp.int32, (nchunks, V7X_LANES), 1)
    cd_lhs = jnp.where(lane_n < period, cd3[0], jnp.where(lane_n < 2 * period, cd3[1], cd3[2]))
    cdec_ref[...] = jnp.dot(cd_lhs, expand3, preferred_element_type=F32)

    def local_states(c, carry):
        rows = pl.ds(pl.multiple_of(c * t, t), t)
        colval = colval_ref[rows, :]
        tot = jnp.where(is_fwd[:1], colval[t - 1:t, :], colval[0:1, :])
        wexp = expand_cols(jnp.exp2(tot - colval) * dts_ref[rows, :])
        x = x_ref[rows, :].astype(F32)
        xd = jnp.concatenate([x * wexp[:, :gch], x * wexp[:, gch:]], axis=1).astype(BF16)
        loc = lax.dot_general(b_ref[rows, :], xd, (((0,), (0,)), ((), ())), preferred_element_type=F32)
        s_ref[c] = loc.astype(BF16)
        return carry

    lax.fori_loop(0, nchunks, local_states, 0, unroll=4)

    def scan(lanes, reverse):
        carry_ref[...] = jnp.zeros_like(carry_ref)

        def body(i, carry):
            c = (nchunks - 1 - i) if reverse else i
            loc = s_ref[c, :, lanes].astype(F32)
            prev = carry_ref[...]
            s_ref[c, :, lanes] = prev.astype(BF16)
            carry_ref[...] = prev * cdec_ref[pl.ds(c, 1), lanes] + loc
            return carry

        lax.fori_loop(0, nchunks, body, 0, unroll=2)

    scan(slice(0, gch), False)
    scan(slice(gch, 2 * gch), True)

    dvec = d_ref[...]
    nw = nw_ref[...]
    zero = jnp.zeros((), BF16)

    def outputs(c, carry):
        rows = pl.ds(pl.multiple_of(c * t, t), t)
        colval = colval_ref[rows, :]
        rowval = rowval_ref[:, rows]
        dtr = dtrow_ref[:, rows]
        xb = x_ref[rows, :]
        bm = b_ref[rows, :]
        cm = c_ref[rows, :]
        cs = jnp.dot(cm, s_ref[c], preferred_element_type=F32) * expand_cols(jnp.exp2(colval))
        y = cs[:, :gch] + cs[:, gch:] + xb.astype(F32) * dvec
        cb = lax.dot_general(cm, bm, (((1,), (1,)), ((), ())), preferred_element_type=F32)
        ys = []
        for p in range(npairs):
            ms = []
            for q in range(2):
                j = 2 * p + q
                e = jnp.where(lower, colval[:, j:j + 1] - rowval[j:j + 1, :],
                              colval[:, hpg + j:hpg + j + 1] - rowval[hpg + j:hpg + j + 1, :])
                wgt = jnp.where(lower, dtr[j:j + 1, :], 0.0) + jnp.where(upper, dtr[hpg + j:hpg + j + 1, :], 0.0)
                ms.append(cb * jnp.exp2(e) * wgt)
            xp = xb[:, p * V7X_LANES:(p + 1) * V7X_LANES]
            rhs = jnp.concatenate([jnp.where(lo_half, xp, zero), jnp.where(lo_half, zero, xp)], axis=0)
            ys.append(jnp.dot(jnp.concatenate(ms, axis=1).astype(BF16), rhs, preferred_element_type=F32))
        y = y + jnp.concatenate(ys, axis=1)
        y = y * _silu(z_ref[rows, :].astype(F32))
        ms = jnp.mean(y * y, axis=-1, keepdims=True)
        y_ref[rows, :] = (y * lax.rsqrt(ms + NORM_EPS) * nw).astype(y_ref.dtype)
        return carry

    lax.fori_loop(0, nchunks, outputs, 0, unroll=2)


def _ssd(zx, xc, dt_raw, dt_bias, a_log, d_skip, norm_w, d_inner, heads):
    bsz, seq_len, _ = xc.shape
    g = SSM_GROUPS
    hpg = heads // g
    gch = d_inner // g
    state = (xc.shape[-1] - d_inner) // (2 * g)
    assert state == V7X_LANES
    nchunks = seq_len // SSM_CHUNK
    reps = V7X_LANES // (2 * hpg)

    def per_group(p, lead):
        p = p.astype(F32).reshape(lead + (2, g, hpg))
        p = jnp.moveaxis(p, -3, -2).reshape(lead + (g, 2 * hpg))
        return jnp.tile(p, (1,) * (len(lead) + 1) + (reps,))

    dt_t = jnp.moveaxis(per_group(dt_raw, (bsz, seq_len)), 2, 1)
    bias_g = per_group(dt_bias, ()).reshape(g, 1, V7X_LANES)
    alog_g = per_group(a_log, ()).reshape(g, 1, V7X_LANES)
    d_exp = jnp.repeat(d_skip.astype(F32), gch // hpg).reshape(g, 1, gch)
    nw = norm_w.astype(F32).reshape(g, 1, gch)
    xoff = d_inner // V7X_LANES
    block_bytes = (2 * 3 * seq_len * gch * 2 + 2 * 2 * seq_len * state * 2 + 2 * seq_len * V7X_LANES * 4
                   + nchunks * state * 2 * gch * 2 + 2 * seq_len * V7X_LANES * 4 + 4 * 2 * hpg * seq_len * 4
                   + nchunks * 2 * gch * 4 + state * gch * 4)
    return pl.pallas_call(
        functools.partial(_ssd_kernel, hpg=hpg),
        grid=(bsz, g),
        in_specs=[
            pl.BlockSpec((None, seq_len, gch), lambda i, j: (i, 0, j)),
            pl.BlockSpec((None, seq_len, gch), lambda i, j: (i, 0, j)),
            pl.BlockSpec((None, seq_len, state), lambda i, j: (i, 0, xoff + j)),
            pl.BlockSpec((None, seq_len, state), lambda i, j: (i, 0, xoff + g + j)),
            pl.BlockSpec((None, None, seq_len, V7X_LANES), lambda i, j: (i, j, 0, 0)),
            pl.BlockSpec((None, 1, V7X_LANES), lambda i, j: (j, 0, 0)),
            pl.BlockSpec((None, 1, V7X_LANES), lambda i, j: (j, 0, 0)),
            pl.BlockSpec((None, 1, gch), lambda i, j: (j, 0, 0)),
            pl.BlockSpec((None, 1, gch), lambda i, j: (j, 0, 0)),
        ],
        out_specs=pl.BlockSpec((None, seq_len, gch), lambda i, j: (i, 0, j)),
        out_shape=jax.ShapeDtypeStruct((bsz, seq_len, d_inner), BF16),
        scratch_shapes=[
            pltpu.VMEM((nchunks, state, 2 * gch), BF16),
            pltpu.VMEM((seq_len, V7X_LANES), F32),
            pltpu.VMEM((seq_len, V7X_LANES), F32),
            pltpu.VMEM((2 * hpg, seq_len), F32),
            pltpu.VMEM((2 * hpg, seq_len), F32),
            pltpu.VMEM((nchunks, V7X_LANES), F32),
            pltpu.VMEM((nchunks, 2 * gch), F32),
            pltpu.VMEM((state, gch), F32),
        ],
        compiler_params=_params(("parallel", "parallel"), block_bytes),
        name="ssd_scan",
    )(zx, xc, xc, xc, dt_t, bias_g, alog_g, d_exp, nw)


QK_ROWS = 512


def _qk_prep_kernel(q_ref, k_ref, qw_ref, kw_ref, cos_ref, sin_ref, qo_ref, ko_ref, *, q_scale):
    dh = cos_ref.shape[-1]
    cos = cos_ref[...]
    sin = sin_ref[...]
    lane = lax.broadcasted_iota(jnp.int32, cos.shape, 1)
    assert dh & (dh - 1) == 0
    first = (lane & (dh // 2 - 1)) < (dh // 4)

    def prep(x, w, scale):
        x = x.astype(F32)
        ms = jnp.mean(x * x, axis=-1, keepdims=True)
        x = x * lax.rsqrt(ms + NORM_EPS) * w
        partner = jnp.where(first, pltpu.roll(x, dh - dh // 4, 1), pltpu.roll(x, dh // 4, 1))
        out = x * cos + partner * sin
        return out * scale if scale != 1.0 else out

    for h in range(q_ref.shape[-1] // dh):
        sl = slice(h * dh, (h + 1) * dh)
        qo_ref[:, sl] = prep(q_ref[:, sl], qw_ref[...], q_scale).astype(qo_ref.dtype)
    for h in range(k_ref.shape[-1] // dh):
        sl = slice(h * dh, (h + 1) * dh)
        ko_ref[:, sl] = prep(k_ref[:, sl], kw_ref[...], 1.0).astype(ko_ref.dtype)


def _qk_prep(qkv, q_norm_w, k_norm_w, cos_full, sin_signed, n_heads, q_scale):
    bsz, seq_len, _ = qkv.shape
    dh = q_norm_w.shape[0]
    qd, kd = n_heads * dh, ATT_KV_HEADS * dh
    assert qd % kd == 0
    tr = _largest_tile(seq_len, QK_ROWS, BF16_SUBLANES)
    block_bytes = 4 * tr * (qd + kd) * 2 + 16 * tr * dh * 4
    return pl.pallas_call(
        functools.partial(_qk_prep_kernel, q_scale=q_scale),
        grid=(bsz, seq_len // tr),
        in_specs=[
            pl.BlockSpec((None, tr, qd), lambda i, j: (i, j, 0)),
            pl.BlockSpec((None, tr, kd), lambda i, j: (i, j, qd // kd)),
            pl.BlockSpec((1, dh), lambda i, j: (0, 0)),
            pl.BlockSpec((1, dh), lambda i, j: (0, 0)),
            pl.BlockSpec((tr, dh), lambda i, j: (j, 0)),
            pl.BlockSpec((tr, dh), lambda i, j: (j, 0)),
        ],
        out_specs=[
            pl.BlockSpec((None, tr, qd), lambda i, j: (i, j, 0)),
            pl.BlockSpec((None, tr, kd), lambda i, j: (i, j, 0)),
        ],
        out_shape=[jax.ShapeDtypeStruct((bsz, seq_len, qd), BF16), jax.ShapeDtypeStruct((bsz, seq_len, kd), BF16)],
        compiler_params=_params(("parallel", "parallel"), block_bytes),
        name="qk_prep",
    )(qkv, qkv, q_norm_w.reshape(1, dh).astype(F32), k_norm_w.reshape(1, dh).astype(F32), cos_full, sin_signed)


ATT_Q_ROWS = 128


def _attention_kernel(q_ref, k_ref, v_ref, o_ref):
    tq, width = q_ref.shape
    dh = k_ref.shape[-1]
    rep = width // dh
    q = q_ref[...]
    qs = jnp.concatenate([q[:, r * dh:(r + 1) * dh] for r in range(rep)], axis=0)
    s = lax.dot_general(k_ref[...], qs, (((1,), (1,)), ((), ())), preferred_element_type=F32)
    m = jnp.max(s, axis=0, keepdims=True)
    p = jnp.exp2(s - m)
    denom = jnp.sum(p, axis=0, keepdims=True)
    acc = lax.dot_general(v_ref[...], p.astype(v_ref.dtype), (((0,), (0,)), ((), ())),
                          preferred_element_type=F32)
    out = (acc * (1.0 / denom)).T
    for r in range(rep):
        o_ref[:, r * dh:(r + 1) * dh] = out[r * tq:(r + 1) * tq, :].astype(o_ref.dtype)


def _attention(q, k, qkv, n_heads):
    bsz, seq_len, qd = q.shape
    dh = qd // n_heads
    rep = n_heads // ATT_KV_HEADS
    width = rep * dh
    voff = (n_heads + ATT_KV_HEADS)
    tq = ATT_Q_ROWS
    block_bytes = (4 * tq * width * 2 + 4 * seq_len * dh * 2 + seq_len * rep * tq * (4 + 4 + 2)
                   + 4 * rep * tq * dh * 4)
    return pl.pallas_call(
        _attention_kernel,
        grid=(bsz, ATT_KV_HEADS, seq_len // tq),
        in_specs=[
            pl.BlockSpec((None, tq, width), lambda b, h, i: (b, i, h)),
            pl.BlockSpec((None, seq_len, dh), lambda b, h, i: (b, 0, h)),
            pl.BlockSpec((None, seq_len, dh), lambda b, h, i: (b, 0, voff + h)),
        ],
        out_specs=pl.BlockSpec((None, tq, width), lambda b, h, i: (b, i, h)),
        out_shape=jax.ShapeDtypeStruct((bsz, seq_len, qd), BF16),
        compiler_params=_params(("parallel", "parallel", "parallel"), block_bytes),
        name="attention",
    )(q, k, qkv)


def _rope_tables(seq_len, dh):
    quarter = dh // 4
    tpos = jnp.arange(seq_len)
    pos = jnp.stack([tpos // GRID_W, tpos % GRID_W], axis=-1).astype(F32)
    inv_freq = ROPE_THETA ** (-jnp.arange(quarter, dtype=F32) / quarter)
    ang = pos[:, :, None] * inv_freq
    cos, sin = jnp.cos(ang), jnp.sin(ang)
    cos_full = jnp.concatenate([cos, cos], axis=-1).reshape(seq_len, dh)
    sin_signed = jnp.concatenate([-sin, sin], axis=-1).reshape(seq_len, dh)
    return cos_full, sin_signed


def _ssd_layer(h, norm_w, w_in, conv_w, conv_b, dt_bias, a_log, d_skip, gn_w, w_out):
    bsz, seq_len, d_model = h.shape
    d_inner = gn_w.shape[0]
    heads = a_log.shape[-1]
    conv_dim = conv_w.shape[-1]
    h2 = h.reshape(bsz * seq_len, d_model)
    w_z = w_in[:, :d_inner].astype(BF16)
    w_xbc = w_in[:, d_inner:d_inner + conv_dim].astype(BF16)
    w_dt = w_in[:, d_inner + conv_dim:].astype(BF16)
    z = _norm_matmul(h2, norm_w, w_z, BF16).reshape(bsz, seq_len, d_inner)
    dt_raw = _norm_matmul(h2, norm_w, w_dt, F32).reshape(bsz, seq_len, 2 * heads)
    xc = _norm_matmul_conv(h2, norm_w, [w_xbc], [conv_w], [conv_b], seq_len,
                           sub_cols=CONV_SUB_COLS).reshape(bsz, seq_len, conv_dim)
    y = _ssd(z, xc, dt_raw, dt_bias, a_log, d_skip, gn_w, d_inner, heads)
    out = _matmul_residual(y.reshape(bsz * seq_len, d_inner), w_out.astype(BF16), h2)
    return out.reshape(bsz, seq_len, d_model)


def _gqa_layer(h, norm_w, w_qkv, q_norm_w, k_norm_w, w_o, tables):
    bsz, seq_len, d_model = h.shape
    dh = q_norm_w.shape[0]
    n_heads = w_o.shape[0] // dh
    h2 = h.reshape(bsz * seq_len, d_model)
    qkv = _norm_matmul(h2, norm_w, w_qkv.astype(BF16), BF16).reshape(bsz, seq_len, -1)
    q, k = _qk_prep(qkv, q_norm_w, k_norm_w, tables[0], tables[1], n_heads, LOG2E * dh ** -0.5)
    o = _attention(q, k, qkv, n_heads)
    out = _matmul_residual(o.reshape(bsz * seq_len, n_heads * dh), w_o.astype(BF16), h2)
    return out.reshape(bsz, seq_len, d_model)


def _ffn_layer(h, norm_w, w_up, conv_w, conv_b, w_down):
    bsz, seq_len, d_model = h.shape
    h2 = h.reshape(bsz * seq_len, d_model)
    f = w_down.shape[0]
    gated = _norm_matmul_conv(h2, norm_w, [w_up[:, :f].astype(BF16), w_up[:, f:].astype(BF16)],
                              [conv_w[:, :f], conv_w[:, f:]], [conv_b[:f], conv_b[f:]], seq_len)
    out = _matmul_residual(gated, w_down.astype(BF16), h2)
    return out.reshape(bsz, seq_len, d_model)


def kernel(x, norm_mix_w, norm_ffn_w, ssm_w_in, ssm_conv_w, ssm_conv_b, ssm_dt_bias, ssm_a_log, ssm_d, ssm_norm_w, ssm_w_out, att_w_qkv, att_q_norm_w, att_k_norm_w, att_w_o, ffn_w_up, ffn_conv_w, ffn_conv_b, ffn_w_down):
    depth = norm_mix_w.shape[0]
    tables = _rope_tables(x.shape[1], att_q_norm_w.shape[-1])
    h = x
    for i in range(depth):
        j = i // N_MIXERS
        if i % N_MIXERS == 0:
            h = _ssd_layer(h, norm_mix_w[i], ssm_w_in[j], ssm_conv_w[j], ssm_conv_b[j], ssm_dt_bias[j],
                           ssm_a_log[j], ssm_d[j], ssm_norm_w[j], ssm_w_out[j])
        else:
            h = _gqa_layer(h, norm_mix_w[i], att_w_qkv[j], att_q_norm_w[j], att_k_norm_w[j], att_w_o[j], tables)
        h = _ffn_layer(h, norm_ffn_w[i], ffn_w_up[i], ffn_conv_w[i], ffn_conv_b[i], ffn_w_down[i])
    return h
```

```python
import functools
import math

import jax
import jax.numpy as jnp
from jax import lax
from jax.experimental import pallas as pl
from jax.experimental.pallas import tpu as pltpu

NORM_EPS = 1e-6
N_MIXERS = 2
SSM_GROUPS = 8
SSM_CHUNK = 128
ATT_KV_HEADS = 4
GRID_W = 64
ROPE_THETA = 10000.0

V7X_VMEM_BYTES = 64 * 1024 * 1024
V7X_LANES = 128
BF16_SUBLANES = 16

F32 = jnp.float32
BF16 = jnp.bfloat16


def _vmem_limit(block_bytes):
    return int(min(V7X_VMEM_BYTES - (6 << 20), max(32 << 20, 2 * block_bytes)))


def _params(semantics, block_bytes, flags=None):
    return pltpu.CompilerParams(dimension_semantics=semantics, vmem_limit_bytes=_vmem_limit(block_bytes), flags=flags)


def _largest_tile(n, cap, quantum):
    best = n if n < quantum else None
    t = quantum
    while t <= min(n, cap):
        if n % t == 0:
            best = t
        t += quantum
    assert best is not None, (n, cap, quantum)
    return best


NORM_ROWS = 128


def _norm_matmul_kernel(h_ref, nw_ref, w_ref, o_ref, a_ref):
    @pl.when(pl.program_id(1) == 0)
    def _():
        def body(r, carry):
            rows = pl.ds(pl.multiple_of(r * NORM_ROWS, NORM_ROWS), NORM_ROWS)
            x = h_ref[rows, :]
            ms = jnp.mean(x * x, axis=-1, keepdims=True)
            a_ref[rows, :] = (x * lax.rsqrt(ms + NORM_EPS) * nw_ref[...]).astype(a_ref.dtype)
            return carry

        lax.fori_loop(0, h_ref.shape[0] // NORM_ROWS, body, 0)

    o_ref[...] = jnp.dot(a_ref[...], w_ref[...], preferred_element_type=F32).astype(o_ref.dtype)


def _norm_matmul(h, norm_w, w, out_dtype):
    m, k = h.shape
    n = w.shape[1]
    tm = _largest_tile(m, 1024, NORM_ROWS)
    tn = _largest_tile(n, 1024, V7X_LANES)
    ob = jnp.dtype(out_dtype).itemsize
    block_bytes = 2 * tm * k * 4 + 2 * k * tn * 2 + 2 * tm * tn * ob + tm * k * 2 + tm * tn * 4
    return pl.pallas_call(
        _norm_matmul_kernel,
        grid=(m // tm, n // tn),
        in_specs=[
            pl.BlockSpec((tm, k), lambda i, j: (i, 0)),
            pl.BlockSpec((1, k), lambda i, j: (0, 0)),
            pl.BlockSpec((k, tn), lambda i, j: (0, j)),
        ],
        out_specs=pl.BlockSpec((tm, tn), lambda i, j: (i, j)),
        out_shape=jax.ShapeDtypeStruct((m, n), out_dtype),
        scratch_shapes=[pltpu.VMEM((tm, k), BF16)],
        compiler_params=_params(("parallel", "arbitrary"), block_bytes),
        name="norm_matmul",
    )(h, norm_w.reshape(1, k).astype(F32), w)


WEIGHT_RESIDENT_BYTES = 8 * 1024 * 1024


def _matmul_residual_kernel(a_ref, w_ref, r_ref, o_ref):
    o_ref[...] = r_ref[...] + jnp.dot(a_ref[...], w_ref[...], preferred_element_type=F32)


def _matmul_residual(a, w, res):
    m, k = a.shape
    n = w.shape[1]
    if k * n * 2 <= WEIGHT_RESIDENT_BYTES:
        tm, tn, cols_outer = _largest_tile(m, 512, V7X_LANES), n, True
    elif k <= 4096:
        tm, tn, cols_outer = _largest_tile(m, 1024, V7X_LANES), _largest_tile(n, 512, V7X_LANES), False
    else:
        tm, tn, cols_outer = _largest_tile(m, 512, V7X_LANES), _largest_tile(n, 1024, V7X_LANES), True
    block_bytes = 2 * tm * k * 2 + 2 * k * tn * 2 + 4 * tm * tn * 4 + tm * tn * 4
    if cols_outer:
        grid = (n // tn, m // tm)
        row_col = lambda j, i: (i, j)
    else:
        grid = (m // tm, n // tn)
        row_col = lambda i, j: (i, j)
    return pl.pallas_call(
        _matmul_residual_kernel,
        grid=grid,
        in_specs=[
            pl.BlockSpec((tm, k), lambda *g: (row_col(*g)[0], 0)),
            pl.BlockSpec((k, tn), lambda *g: (0, row_col(*g)[1])),
            pl.BlockSpec((tm, tn), lambda *g: row_col(*g)),
        ],
        out_specs=pl.BlockSpec((tm, tn), lambda *g: row_col(*g)),
        out_shape=jax.ShapeDtypeStruct((m, n), F32),
        compiler_params=_params(("parallel", "parallel"), block_bytes),
        name="matmul_residual",
    )(a, w, res)


CONV_ROWS = 256
CONV_HALO = BF16_SUBLANES
CONV_SUB_COLS = 256


def _silu(x):
    half = 0.5 * x
    return half + half * jnp.tanh(half)


def _rmsnorm_rows(x, nw):
    ms = jnp.mean(x * x, axis=-1, keepdims=True)
    return (x * lax.rsqrt(ms + NORM_EPS) * nw).astype(BF16)


def _norm_matmul_conv_kernel(*refs, branches, tiles_per_seq):
    h_ref, hp_ref, hn_ref, nw_ref = refs[:4]
    w_refs = refs[4:4 + branches]
    cw_refs = refs[4 + branches:4 + 2 * branches]
    cb_refs = refs[4 + 2 * branches:4 + 3 * branches]
    o_ref, a_ref = refs[4 + 3 * branches], refs[5 + 3 * branches]
    u_refs = refs[6 + 3 * branches:]
    i, j = pl.program_id(0), pl.program_id(1)
    tm = h_ref.shape[0]
    halo = CONV_HALO

    @pl.when(j == 0)
    def _():
        nw = nw_ref[...]
        starts_seq = (i % tiles_per_seq) == 0
        ends_seq = (i % tiles_per_seq) == tiles_per_seq - 1
        a_ref[0:halo, :] = jnp.where(starts_seq, 0.0, _rmsnorm_rows(hp_ref[...], nw).astype(F32)).astype(BF16)
        a_ref[halo + tm:, :] = jnp.where(ends_seq, 0.0, _rmsnorm_rows(hn_ref[...], nw).astype(F32)).astype(BF16)

        def body(r, carry):
            rows = pl.multiple_of(r * NORM_ROWS, NORM_ROWS)
            a_ref[pl.ds(halo + rows, NORM_ROWS), :] = _rmsnorm_rows(h_ref[pl.ds(rows, NORM_ROWS), :], nw)
            return carry

        lax.fori_loop(0, tm // NORM_ROWS, body, 0)

    taps = cw_refs[0].shape[0]
    pad = taps // 2
    tn = o_ref.shape[1]
    sub = u_refs[0].shape[-1]
    for s, n0 in enumerate(range(0, tn, sub)):
        cols = slice(n0, n0 + sub)
        slot = s % 2
        for w_ref, u_ref in zip(w_refs, u_refs):
            u_ref[slot] = jnp.dot(a_ref[...], w_ref[:, cols], preferred_element_type=F32)
        for r0 in range(0, tm, CONV_ROWS):
            convs = []
            for u_ref, cw_ref, cb_ref in zip(u_refs, cw_refs, cb_refs):
                acc = cb_ref[:, cols]
                for k in range(taps):
                    start = halo + r0 - pad + k
                    acc = acc + u_ref[slot, start:start + CONV_ROWS, :] * cw_ref[k:k + 1, cols]
                convs.append(acc)
            out = _silu(convs[0]) * convs[1] if branches == 2 else _silu(convs[0])
            o_ref[r0:r0 + CONV_ROWS, cols] = out.astype(o_ref.dtype)


def _norm_matmul_conv(h, norm_w, ws, conv_ws, conv_bs, seq_len, sub_cols=None):
    m, k = h.shape
    n = ws[0].shape[1]
    branches = len(ws)
    tm = _largest_tile(seq_len, 1024, CONV_ROWS)
    tn = _largest_tile(n, 1024 // branches, V7X_LANES)
    sub = tn if sub_cols is None else _largest_tile(tn, sub_cols, V7X_LANES)
    hb = tm // CONV_HALO
    last_hb = m // CONV_HALO - 1
    taps = conv_ws[0].shape[0]
    ext = tm + 2 * CONV_HALO
    block_bytes = (2 * ext * k * 4 + branches * 2 * k * tn * 2 + 2 * tm * tn * 2 + ext * k * 2
                   + 3 * branches * ext * sub * 4 + 8 * CONV_ROWS * tn * 4)
    w_spec = pl.BlockSpec((k, tn), lambda i, j: (0, j))
    cw_spec = pl.BlockSpec((taps, tn), lambda i, j: (0, j))
    cb_spec = pl.BlockSpec((1, tn), lambda i, j: (0, j))
    return pl.pallas_call(
        functools.partial(_norm_matmul_conv_kernel, branches=branches, tiles_per_seq=seq_len // tm),
        grid=(m // tm, n // tn),
        in_specs=[
            pl.BlockSpec((tm, k), lambda i, j: (i, 0)),
            pl.BlockSpec((CONV_HALO, k), lambda i, j: (jnp.maximum(i * hb - 1, 0), 0)),
            pl.BlockSpec((CONV_HALO, k), lambda i, j: (jnp.minimum((i + 1) * hb, last_hb), 0)),
            pl.BlockSpec((1, k), lambda i, j: (0, 0)),
        ] + [w_spec] * branches + [cw_spec] * branches + [cb_spec] * branches,
        out_specs=pl.BlockSpec((tm, tn), lambda i, j: (i, j)),
        out_shape=jax.ShapeDtypeStruct((m, n), BF16),
        scratch_shapes=[pltpu.VMEM((ext, k), BF16)] + [pltpu.VMEM((2, ext, sub), F32)] * branches,
        compiler_params=_params(("parallel", "arbitrary"), block_bytes),
        name="norm_matmul_conv",
    )(h, h, h, norm_w.reshape(1, k).astype(F32), *ws, *[w.astype(F32) for w in conv_ws],
      *[b.reshape(1, n).astype(F32) for b in conv_bs])


LOG2E = math.log2(math.e)


def _split3(x):
    hi = x.astype(BF16)
    r1 = x - hi.astype(F32)
    mid = r1.astype(BF16)
    lo = (r1 - mid.astype(F32)).astype(BF16)
    return hi, mid, lo


def _softplus(x):
    return jnp.maximum(x, 0.0) + jnp.log(1.0 + jnp.exp(-jnp.abs(x)))


def _ssd_kernel(z_ref, x_ref, b_ref, c_ref, dt_ref, bias_ref, alog_ref, d_ref, nw_ref, y_ref,
                s_ref, colval_ref, dts_ref, rowval_ref, tot_ref, cdec_ref, carry_ref, xd_ref, *, hpg):
    t = SSM_CHUNK
    seq_len, gch = x_ref.shape
    nchunks = seq_len // t
    period = 2 * hpg
    hd = gch // hpg
    npairs = gch // V7X_LANES
    assert hd * 2 == V7X_LANES and t == V7X_LANES and 3 * period <= V7X_LANES
    assert hd & (hd - 1) == 0 and period & (period - 1) == 0

    row = lax.broadcasted_iota(jnp.int32, (t, t), 0)
    col = lax.broadcasted_iota(jnp.int32, (t, t), 1)
    lower = row >= col
    tri = jnp.where(lower, 1.0, 0.0).astype(BF16)
    lane = col
    is_fwd = (lane & (period - 1)) < hpg
    lo_half = lane < hd
    e_row = lax.broadcasted_iota(jnp.int32, (V7X_LANES, 2 * gch), 0)
    e_col = lax.broadcasted_iota(jnp.int32, (V7X_LANES, 2 * gch), 1)
    match = (e_row & (period - 1)) == (e_col >> (hd.bit_length() - 1))
    expand2 = jnp.where(match & (e_row < 2 * period), 1.0, 0.0).astype(BF16)
    expand3 = jnp.where(match & (e_row < 3 * period), 1.0, 0.0).astype(BF16)

    def expand_cols(v, cols=slice(None)):
        hi = v.astype(BF16)
        mid = (v - hi.astype(F32)).astype(BF16)
        return jnp.dot(jnp.where(lane[:v.shape[0]] < period, hi, mid), expand2[:, cols],
                       preferred_element_type=F32)

    bias = bias_ref[...]
    a2c = -jnp.exp(alog_ref[...]) * LOG2E

    def scalars(c, carry):
        rows = pl.ds(pl.multiple_of(c * t, t), t)
        dt = _softplus(dt_ref[rows, :] + bias)
        a2 = dt * a2c
        cum3 = jnp.dot(tri, jnp.concatenate(_split3(a2), axis=1), preferred_element_type=F32)
        cum = cum3[:, :t] + cum3[:, t:2 * t] + cum3[:, 2 * t:]
        tot = cum[t - 1:t, :]
        colval = jnp.where(is_fwd, cum, tot - cum + a2)
        colval_ref[rows, :] = colval
        dts_ref[rows, :] = dt
        rowval_ref[:, rows] = (colval - jnp.log2(dt)).T[:period, :]
        tot_ref[pl.ds(c, 1), :] = tot
        return carry

    lax.fori_loop(0, nchunks, scalars, 0, unroll=4)

    cd3 = _split3(jnp.exp2(tot_ref[...]))
    lane_n = lax.broadcasted_iota(jnp.int32, (nchunks, V7X_LANES), 1)
    cd_lhs = jnp.where(lane_n < period, cd3[0], jnp.where(lane_n < 2 * period, cd3[1], cd3[2]))
    cdec_ref[...] = jnp.dot(cd_lhs, expand3, preferred_element_type=F32)

    def weighted_inputs(c, carry):
        rows = pl.ds(pl.multiple_of(c * t, t), t)
        colval = colval_ref[rows, :]
        tot = jnp.where(is_fwd[:1], colval[t - 1:t, :], colval[0:1, :])
        wexp = expand_cols(jnp.exp2(tot - colval) * dts_ref[rows, :])
        x = x_ref[rows, :].astype(F32)
        xd_ref[rows, :] = jnp.concatenate([x * wexp[:, :gch], x * wexp[:, gch:]], axis=1).astype(BF16)
        return carry

    lax.fori_loop(0, nchunks, weighted_inputs, 0, unroll=4)

    def local_states(c, carry):
        rows = pl.ds(pl.multiple_of(c * t, t), t)
        loc = lax.dot_general(b_ref[rows, :], xd_ref[rows, :], (((0,), (0,)), ((), ())),
                              preferred_element_type=F32)
        s_ref[c] = loc.astype(BF16)
        return carry

    lax.fori_loop(0, nchunks, local_states, 0, unroll=4)

    def scan(lanes, reverse):
        carry_ref[...] = jnp.zeros_like(carry_ref)

        def body(i, carry):
            c = (nchunks - 1 - i) if reverse else i
            loc = s_ref[c, :, lanes].astype(F32)
            prev = carry_ref[...]
            s_ref[c, :, lanes] = prev.astype(BF16)
            carry_ref[...] = prev * cdec_ref[pl.ds(c, 1), lanes] + loc
            return carry

        lax.fori_loop(0, nchunks, body, 0, unroll=2)

    scan(slice(0, gch), False)
    scan(slice(gch, 2 * gch), True)

    dvec = d_ref[...]
    nw = nw_ref[...]
    zero = jnp.zeros((), BF16)

    def outputs(c, carry):
        rows = pl.ds(pl.multiple_of(c * t, t), t)
        colval = colval_ref[rows, :]
        rowval = rowval_ref[:, rows]
        xb = x_ref[rows, :]
        bm = b_ref[rows, :]
        cm = c_ref[rows, :]
        cs = jnp.dot(cm, s_ref[c], preferred_element_type=F32) * expand_cols(jnp.exp2(colval))
        cb = lax.dot_general(cm, bm, (((1,), (1,)), ((), ())), preferred_element_type=F32)
        diag = jnp.sum(jnp.where(row == col, cb, 0.0), axis=-1, keepdims=True)
        coef = expand_cols(diag * dts_ref[rows, :], slice(gch, 2 * gch))
        y = cs[:, :gch] + cs[:, gch:] + xb.astype(F32) * (dvec + coef)
        ys = []
        for p in range(npairs):
            ms = []
            for q in range(2):
                j = 2 * p + q
                e = jnp.where(lower, colval[:, j:j + 1] - rowval[j:j + 1, :],
                              colval[:, hpg + j:hpg + j + 1] - rowval[hpg + j:hpg + j + 1, :])
                ms.append(cb * jnp.exp2(e))
            xp = xb[:, p * V7X_LANES:(p + 1) * V7X_LANES]
            rhs = jnp.concatenate([jnp.where(lo_half, xp, zero), jnp.where(lo_half, zero, xp)], axis=0)
            ys.append(jnp.dot(jnp.concatenate(ms, axis=1).astype(BF16), rhs, preferred_element_type=F32))
        y = y + jnp.concatenate(ys, axis=1)
        y = y * _silu(z_ref[rows, :].astype(F32))
        ms = jnp.mean(y * y, axis=-1, keepdims=True)
        y_ref[rows, :] = (y * lax.rsqrt(ms + NORM_EPS) * nw).astype(y_ref.dtype)
        return carry

    lax.fori_loop(0, nchunks, outputs, 0, unroll=4)


def _ssd(zx, xc, dt_raw, dt_bias, a_log, d_skip, norm_w, d_inner, heads):
    bsz, seq_len, _ = xc.shape
    g = SSM_GROUPS
    hpg = heads // g
    gch = d_inner // g
    state = (xc.shape[-1] - d_inner) // (2 * g)
    assert state == V7X_LANES
    nchunks = seq_len // SSM_CHUNK
    reps = V7X_LANES // (2 * hpg)

    def per_group(p, lead):
        p = p.astype(F32).reshape(lead + (2, g, hpg))
        p = jnp.moveaxis(p, -3, -2).reshape(lead + (g, 2 * hpg))
        return jnp.tile(p, (1,) * (len(lead) + 1) + (reps,))

    dt_t = jnp.moveaxis(per_group(dt_raw, (bsz, seq_len)), 2, 1)
    bias_g = per_group(dt_bias, ()).reshape(g, 1, V7X_LANES)
    alog_g = per_group(a_log, ()).reshape(g, 1, V7X_LANES)
    d_exp = jnp.repeat(d_skip.astype(F32), gch // hpg).reshape(g, 1, gch)
    nw = norm_w.astype(F32).reshape(g, 1, gch)
    xoff = d_inner // V7X_LANES
    block_bytes = (2 * 3 * seq_len * gch * 2 + 2 * 2 * seq_len * state * 2 + 2 * seq_len * V7X_LANES * 4
                   + nchunks * state * 2 * gch * 2 + 2 * seq_len * V7X_LANES * 4 + 2 * 2 * hpg * seq_len * 4
                   + nchunks * 2 * gch * 4 + state * gch * 4 + seq_len * 2 * gch * 2)
    return pl.pallas_call(
        functools.partial(_ssd_kernel, hpg=hpg),
        grid=(bsz, g),
        in_specs=[
            pl.BlockSpec((None, seq_len, gch), lambda i, j: (i, 0, j)),
            pl.BlockSpec((None, seq_len, gch), lambda i, j: (i, 0, j)),
            pl.BlockSpec((None, seq_len, state), lambda i, j: (i, 0, xoff + j)),
            pl.BlockSpec((None, seq_len, state), lambda i, j: (i, 0, xoff + g + j)),
            pl.BlockSpec((None, None, seq_len, V7X_LANES), lambda i, j: (i, j, 0, 0)),
            pl.BlockSpec((None, 1, V7X_LANES), lambda i, j: (j, 0, 0)),
            pl.BlockSpec((None, 1, V7X_LANES), lambda i, j: (j, 0, 0)),
            pl.BlockSpec((None, 1, gch), lambda i, j: (j, 0, 0)),
            pl.BlockSpec((None, 1, gch), lambda i, j: (j, 0, 0)),
        ],
        out_specs=pl.BlockSpec((None, seq_len, gch), lambda i, j: (i, 0, j)),
        out_shape=jax.ShapeDtypeStruct((bsz, seq_len, d_inner), BF16),
        scratch_shapes=[
            pltpu.VMEM((nchunks, state, 2 * gch), BF16),
            pltpu.VMEM((seq_len, V7X_LANES), F32),
            pltpu.VMEM((seq_len, V7X_LANES), F32),
            pltpu.VMEM((2 * hpg, seq_len), F32),
            pltpu.VMEM((nchunks, V7X_LANES), F32),
            pltpu.VMEM((nchunks, 2 * gch), F32),
            pltpu.VMEM((state, gch), F32),
            pltpu.VMEM((seq_len, 2 * gch), BF16),
        ],
        compiler_params=_params(("parallel", "parallel"), block_bytes),
        name="ssd_scan",
    )(zx, xc, xc, xc, dt_t, bias_g, alog_g, d_exp, nw)


QK_ROWS = 512


def _qk_prep_kernel(q_ref, k_ref, qw_ref, kw_ref, cos_ref, sin_ref, qo_ref, ko_ref, *, q_scale):
    dh = cos_ref.shape[-1]
    cos = cos_ref[...]
    sin = sin_ref[...]
    lane = lax.broadcasted_iota(jnp.int32, cos.shape, 1)
    assert dh & (dh - 1) == 0
    first = (lane & (dh // 2 - 1)) < (dh // 4)

    def prep(x, w, scale):
        x = x.astype(F32)
        ms = jnp.mean(x * x, axis=-1, keepdims=True)
        x = x * lax.rsqrt(ms + NORM_EPS) * w
        partner = jnp.where(first, pltpu.roll(x, dh - dh // 4, 1), pltpu.roll(x, dh // 4, 1))
        out = x * cos + partner * sin
        return out * scale if scale != 1.0 else out

    for h in range(q_ref.shape[-1] // dh):
        sl = slice(h * dh, (h + 1) * dh)
        qo_ref[:, sl] = prep(q_ref[:, sl], qw_ref[...], q_scale).astype(qo_ref.dtype)
    for h in range(k_ref.shape[-1] // dh):
        sl = slice(h * dh, (h + 1) * dh)
        ko_ref[:, sl] = prep(k_ref[:, sl], kw_ref[...], 1.0).astype(ko_ref.dtype)


def _qk_prep(qkv, q_norm_w, k_norm_w, cos_full, sin_signed, n_heads, q_scale):
    bsz, seq_len, _ = qkv.shape
    dh = q_norm_w.shape[0]
    qd, kd = n_heads * dh, ATT_KV_HEADS * dh
    assert qd % kd == 0
    tr = _largest_tile(seq_len, QK_ROWS, BF16_SUBLANES)
    block_bytes = 4 * tr * (qd + kd) * 2 + 16 * tr * dh * 4
    return pl.pallas_call(
        functools.partial(_qk_prep_kernel, q_scale=q_scale),
        grid=(bsz, seq_len // tr),
        in_specs=[
            pl.BlockSpec((None, tr, qd), lambda i, j: (i, j, 0)),
            pl.BlockSpec((None, tr, kd), lambda i, j: (i, j, qd // kd)),
            pl.BlockSpec((1, dh), lambda i, j: (0, 0)),
            pl.BlockSpec((1, dh), lambda i, j: (0, 0)),
            pl.BlockSpec((tr, dh), lambda i, j: (j, 0)),
            pl.BlockSpec((tr, dh), lambda i, j: (j, 0)),
        ],
        out_specs=[
            pl.BlockSpec((None, tr, qd), lambda i, j: (i, j, 0)),
            pl.BlockSpec((None, tr, kd), lambda i, j: (i, j, 0)),
        ],
        out_shape=[jax.ShapeDtypeStruct((bsz, seq_len, qd), BF16), jax.ShapeDtypeStruct((bsz, seq_len, kd), BF16)],
        compiler_params=_params(("parallel", "parallel"), block_bytes),
        name="qk_prep",
    )(qkv, qkv, q_norm_w.reshape(1, dh).astype(F32), k_norm_w.reshape(1, dh).astype(F32), cos_full, sin_signed)


ATT_Q_ROWS = 128


def _attention_kernel(q_ref, k_ref, v_ref, o_ref):
    tq, width = q_ref.shape
    dh = k_ref.shape[-1]
    rep = width // dh
    q = q_ref[...]
    qs = jnp.concatenate([q[:, r * dh:(r + 1) * dh] for r in range(rep)], axis=0)
    s = lax.dot_general(k_ref[...], qs, (((1,), (1,)), ((), ())), preferred_element_type=F32)
    m = jnp.max(s, axis=0, keepdims=True)
    p = jnp.exp2(s - m)
    denom = jnp.sum(p, axis=0, keepdims=True)
    acc = lax.dot_general(v_ref[...], p.astype(v_ref.dtype), (((0,), (0,)), ((), ())),
                          preferred_element_type=F32)
    out = (acc * (1.0 / denom)).T
    for r in range(rep):
        o_ref[:, r * dh:(r + 1) * dh] = out[r * tq:(r + 1) * tq, :].astype(o_ref.dtype)


def _attention(q, k, qkv, n_heads):
    bsz, seq_len, qd = q.shape
    dh = qd // n_heads
    rep = n_heads // ATT_KV_HEADS
    width = rep * dh
    voff = (n_heads + ATT_KV_HEADS)
    tq = ATT_Q_ROWS
    block_bytes = (4 * tq * width * 2 + 4 * seq_len * dh * 2 + seq_len * rep * tq * (4 + 4 + 2)
                   + 4 * rep * tq * dh * 4)
    return pl.pallas_call(
        _attention_kernel,
        grid=(bsz, ATT_KV_HEADS, seq_len // tq),
        in_specs=[
            pl.BlockSpec((None, tq, width), lambda b, h, i: (b, i, h)),
            pl.BlockSpec((None, seq_len, dh), lambda b, h, i: (b, 0, h)),
            pl.BlockSpec((None, seq_len, dh), lambda b, h, i: (b, 0, voff + h)),
        ],
        out_specs=pl.BlockSpec((None, tq, width), lambda b, h, i: (b, i, h)),
        out_shape=jax.ShapeDtypeStruct((bsz, seq_len, qd), BF16),
        compiler_params=_params(("parallel", "parallel", "parallel"), block_bytes),
        name="attention",
    )(q, k, qkv)


def _rope_tables(seq_len, dh):
    quarter = dh // 4
    tpos = jnp.arange(seq_len)
    pos = jnp.stack([tpos // GRID_W, tpos % GRID_W], axis=-1).astype(F32)
    inv_freq = ROPE_THETA ** (-jnp.arange(quarter, dtype=F32) / quarter)
    ang = pos[:, :, None] * inv_freq
    cos, sin = jnp.cos(ang), jnp.sin(ang)
    cos_full = jnp.concatenate([cos, cos], axis=-1).reshape(seq_len, dh)
    sin_signed = jnp.concatenate([-sin, sin], axis=-1).reshape(seq_len, dh)
    return cos_full, sin_signed


def _ssd_layer(h, norm_w, w_in, conv_w, conv_b, dt_bias, a_log, d_skip, gn_w, w_out):
    bsz, seq_len, d_model = h.shape
    d_inner = gn_w.shape[0]
    heads = a_log.shape[-1]
    conv_dim = conv_w.shape[-1]
    h2 = h.reshape(bsz * seq_len, d_model)
    w_z = w_in[:, :d_inner].astype(BF16)
    w_xbc = w_in[:, d_inner:d_inner + conv_dim].astype(BF16)
    w_dt = w_in[:, d_inner + conv_dim:].astype(BF16)
    z = _norm_matmul(h2, norm_w, w_z, BF16).reshape(bsz, seq_len, d_inner)
    dt_raw = _norm_matmul(h2, norm_w, w_dt, F32).reshape(bsz, seq_len, 2 * heads)
    xc = _norm_matmul_conv(h2, norm_w, [w_xbc], [conv_w], [conv_b], seq_len,
                           sub_cols=CONV_SUB_COLS).reshape(bsz, seq_len, conv_dim)
    y = _ssd(z, xc, dt_raw, dt_bias, a_log, d_skip, gn_w, d_inner, heads)
    out = _matmul_residual(y.reshape(bsz * seq_len, d_inner), w_out.astype(BF16), h2)
    return out.reshape(bsz, seq_len, d_model)


def _gqa_layer(h, norm_w, w_qkv, q_norm_w, k_norm_w, w_o, tables):
    bsz, seq_len, d_model = h.shape
    dh = q_norm_w.shape[0]
    n_heads = w_o.shape[0] // dh
    h2 = h.reshape(bsz * seq_len, d_model)
    qkv = _norm_matmul(h2, norm_w, w_qkv.astype(BF16), BF16).reshape(bsz, seq_len, -1)
    q, k = _qk_prep(qkv, q_norm_w, k_norm_w, tables[0], tables[1], n_heads, LOG2E * dh ** -0.5)
    o = _attention(q, k, qkv, n_heads)
    out = _matmul_residual(o.reshape(bsz * seq_len, n_heads * dh), w_o.astype(BF16), h2)
    return out.reshape(bsz, seq_len, d_model)


def _ffn_layer(h, norm_w, w_up, conv_w, conv_b, w_down):
    bsz, seq_len, d_model = h.shape
    h2 = h.reshape(bsz * seq_len, d_model)
    f = w_down.shape[0]
    gated = _norm_matmul_conv(h2, norm_w, [w_up[:, :f].astype(BF16), w_up[:, f:].astype(BF16)],
                              [conv_w[:, :f], conv_w[:, f:]], [conv_b[:f], conv_b[f:]], seq_len)
    out = _matmul_residual(gated, w_down.astype(BF16), h2)
    return out.reshape(bsz, seq_len, d_model)


def kernel(x, norm_mix_w, norm_ffn_w, ssm_w_in, ssm_conv_w, ssm_conv_b, ssm_dt_bias, ssm_a_log, ssm_d, ssm_norm_w, ssm_w_out, att_w_qkv, att_q_norm_w, att_k_norm_w, att_w_o, ffn_w_up, ffn_conv_w, ffn_conv_b, ffn_w_down):
    depth = norm_mix_w.shape[0]
    tables = _rope_tables(x.shape[1], att_q_norm_w.shape[-1])
    h = x
    for i in range(depth):
        j = i // N_MIXERS
        if i % N_MIXERS == 0:
            h = _ssd_layer(h, norm_mix_w[i], ssm_w_in[j], ssm_conv_w[j], ssm_conv_b[j], ssm_dt_bias[j],
                           ssm_a_log[j], ssm_d[j], ssm_norm_w[j], ssm_w_out[j])
        else:
            h = _gqa_layer(h, norm_mix_w[i], att_w_qkv[j], att_q_norm_w[j], att_k_norm_w[j], att_w_o[j], tables)
        h = _ffn_layer(h, norm_ffn_w[i], ffn_w_up[i], ffn_conv_w[i], ffn_conv_b[i], ffn_w_down[i])
    return h
```

```python
import functools
import math

import jax
import jax.numpy as jnp
from jax import lax
from jax.experimental import pallas as pl
from jax.experimental.pallas import tpu as pltpu

NORM_EPS = 1e-6
N_MIXERS = 2
SSM_GROUPS = 8
SSM_CHUNK = 128
ATT_KV_HEADS = 4
GRID_W = 64
ROPE_THETA = 10000.0

V7X_VMEM_BYTES = 64 * 1024 * 1024
V7X_LANES = 128
BF16_SUBLANES = 16

F32 = jnp.float32
BF16 = jnp.bfloat16


def _vmem_limit(block_bytes):
    return int(min(V7X_VMEM_BYTES - (6 << 20), max(32 << 20, 2 * block_bytes)))


def _params(semantics, block_bytes, flags=None):
    return pltpu.CompilerParams(dimension_semantics=semantics, vmem_limit_bytes=_vmem_limit(block_bytes), flags=flags)


def _largest_tile(n, cap, quantum):
    best = n if n < quantum else None
    t = quantum
    while t <= min(n, cap):
        if n % t == 0:
            best = t
        t += quantum
    assert best is not None, (n, cap, quantum)
    return best


NORM_ROWS = 128


def _norm_matmul_kernel(*refs, side):
    if side:
        h_ref, nw_ref, w_ref, ws_ref, o_ref, os_ref, a_ref = refs
    else:
        h_ref, nw_ref, w_ref, o_ref, a_ref = refs

    @pl.when(pl.program_id(1) == 0)
    def _():
        def body(r, carry):
            rows = pl.ds(pl.multiple_of(r * NORM_ROWS, NORM_ROWS), NORM_ROWS)
            x = h_ref[rows, :]
            ms = jnp.mean(x * x, axis=-1, keepdims=True)
            a_ref[rows, :] = (x * lax.rsqrt(ms + NORM_EPS) * nw_ref[...]).astype(a_ref.dtype)
            return carry

        lax.fori_loop(0, h_ref.shape[0] // NORM_ROWS, body, 0)
        if side:
            os_ref[...] = jnp.dot(a_ref[...], ws_ref[...], preferred_element_type=F32)

    o_ref[...] = jnp.dot(a_ref[...], w_ref[...], preferred_element_type=F32).astype(o_ref.dtype)


def _norm_matmul(h, norm_w, w, out_dtype, w_side=None):
    m, k = h.shape
    n = w.shape[1]
    tm = _largest_tile(m, 1024, NORM_ROWS)
    tn = _largest_tile(n, 1024, V7X_LANES)
    ob = jnp.dtype(out_dtype).itemsize
    block_bytes = 2 * tm * k * 4 + 2 * k * tn * 2 + 2 * tm * tn * ob + tm * k * 2 + tm * tn * 4
    in_specs = [
        pl.BlockSpec((tm, k), lambda i, j: (i, 0)),
        pl.BlockSpec((1, k), lambda i, j: (0, 0)),
        pl.BlockSpec((k, tn), lambda i, j: (0, j)),
    ]
    out_specs = [pl.BlockSpec((tm, tn), lambda i, j: (i, j))]
    out_shape = [jax.ShapeDtypeStruct((m, n), out_dtype)]
    operands = [h, norm_w.reshape(1, k).astype(F32), w]
    if w_side is not None:
        ns = w_side.shape[1]
        in_specs.append(pl.BlockSpec((k, ns), lambda i, j: (0, 0)))
        out_specs.append(pl.BlockSpec((tm, ns), lambda i, j: (i, 0)))
        out_shape.append(jax.ShapeDtypeStruct((m, ns), F32))
        operands.append(w_side)
        block_bytes += 2 * k * ns * 2 + 3 * tm * ns * 4
    outs = pl.pallas_call(
        functools.partial(_norm_matmul_kernel, side=w_side is not None),
        grid=(m // tm, n // tn),
        in_specs=in_specs,
        out_specs=out_specs,
        out_shape=out_shape,
        scratch_shapes=[pltpu.VMEM((tm, k), BF16)],
        compiler_params=_params(("parallel", "arbitrary"), block_bytes),
        name="norm_matmul",
    )(*operands)
    return outs if w_side is not None else outs[0]


WEIGHT_RESIDENT_BYTES = 8 * 1024 * 1024


def _matmul_residual_kernel(a_ref, w_ref, r_ref, o_ref):
    o_ref[...] = r_ref[...] + jnp.dot(a_ref[...], w_ref[...], preferred_element_type=F32)


def _matmul_residual(a, w, res):
    m, k = a.shape
    n = w.shape[1]
    if k * n * 2 <= WEIGHT_RESIDENT_BYTES:
        tm, tn, cols_outer = _largest_tile(m, 512, V7X_LANES), n, True
    elif k < 4096:
        tm, tn, cols_outer = _largest_tile(m, 1024, V7X_LANES), _largest_tile(n, 512, V7X_LANES), False
    else:
        tm, tn, cols_outer = _largest_tile(m, 512, V7X_LANES), _largest_tile(n, 1024, V7X_LANES), True
    block_bytes = 2 * tm * k * 2 + 2 * k * tn * 2 + 4 * tm * tn * 4 + tm * tn * 4
    if cols_outer:
        grid = (n // tn, m // tm)
        row_col = lambda j, i: (i, j)
    else:
        grid = (m // tm, n // tn)
        row_col = lambda i, j: (i, j)
    return pl.pallas_call(
        _matmul_residual_kernel,
        grid=grid,
        in_specs=[
            pl.BlockSpec((tm, k), lambda *g: (row_col(*g)[0], 0)),
            pl.BlockSpec((k, tn), lambda *g: (0, row_col(*g)[1])),
            pl.BlockSpec((tm, tn), lambda *g: row_col(*g)),
        ],
        out_specs=pl.BlockSpec((tm, tn), lambda *g: row_col(*g)),
        out_shape=jax.ShapeDtypeStruct((m, n), F32),
        compiler_params=_params(("parallel", "parallel"), block_bytes),
        name="matmul_residual",
    )(a, w, res)


CONV_ROWS = 256
CONV_HALO = BF16_SUBLANES
CONV_SUB_COLS = 256


def _silu(x):
    half = 0.5 * x
    return half + half * jnp.tanh(half)


def _rmsnorm_rows(x, nw):
    ms = jnp.mean(x * x, axis=-1, keepdims=True)
    return (x * lax.rsqrt(ms + NORM_EPS) * nw).astype(BF16)


def _norm_matmul_conv_kernel(*refs, branches, tiles_per_seq):
    h_ref, hp_ref, hn_ref, nw_ref = refs[:4]
    w_refs = refs[4:4 + branches]
    cw_refs = refs[4 + branches:4 + 2 * branches]
    cb_refs = refs[4 + 2 * branches:4 + 3 * branches]
    o_ref, a_ref = refs[4 + 3 * branches], refs[5 + 3 * branches]
    u_refs = refs[6 + 3 * branches:]
    i, j = pl.program_id(0), pl.program_id(1)
    tm = h_ref.shape[0]
    halo = CONV_HALO

    @pl.when(j == 0)
    def _():
        nw = nw_ref[...]
        starts_seq = (i % tiles_per_seq) == 0
        ends_seq = (i % tiles_per_seq) == tiles_per_seq - 1
        a_ref[0:halo, :] = jnp.where(starts_seq, 0.0, _rmsnorm_rows(hp_ref[...], nw).astype(F32)).astype(BF16)
        a_ref[halo + tm:, :] = jnp.where(ends_seq, 0.0, _rmsnorm_rows(hn_ref[...], nw).astype(F32)).astype(BF16)

        def body(r, carry):
            rows = pl.multiple_of(r * NORM_ROWS, NORM_ROWS)
            a_ref[pl.ds(halo + rows, NORM_ROWS), :] = _rmsnorm_rows(h_ref[pl.ds(rows, NORM_ROWS), :], nw)
            return carry

        lax.fori_loop(0, tm // NORM_ROWS, body, 0)

    taps = cw_refs[0].shape[0]
    pad = taps // 2
    tn = o_ref.shape[1]
    sub = u_refs[0].shape[-1]
    for s, n0 in enumerate(range(0, tn, sub)):
        cols = slice(n0, n0 + sub)
        slot = s % 2
        for w_ref, u_ref in zip(w_refs, u_refs):
            u_ref[slot] = jnp.dot(a_ref[...], w_ref[:, cols], preferred_element_type=F32)
        for r0 in range(0, tm, CONV_ROWS):
            convs = []
            for u_ref, cw_ref, cb_ref in zip(u_refs, cw_refs, cb_refs):
                acc = cb_ref[:, cols]
                for k in range(taps):
                    start = halo + r0 - pad + k
                    acc = acc + u_ref[slot, start:start + CONV_ROWS, :] * cw_ref[k:k + 1, cols]
                convs.append(acc)
            out = _silu(convs[0]) * convs[1] if branches == 2 else _silu(convs[0])
            o_ref[r0:r0 + CONV_ROWS, cols] = out.astype(o_ref.dtype)


def _norm_matmul_conv(h, norm_w, ws, conv_ws, conv_bs, seq_len, sub_cols=None):
    m, k = h.shape
    n = ws[0].shape[1]
    branches = len(ws)
    tm = _largest_tile(seq_len, 1024, CONV_ROWS)
    tn = _largest_tile(n, 1024 // branches, V7X_LANES)
    sub = tn if sub_cols is None else _largest_tile(tn, sub_cols, V7X_LANES)
    hb = tm // CONV_HALO
    last_hb = m // CONV_HALO - 1
    taps = conv_ws[0].shape[0]
    ext = tm + 2 * CONV_HALO
    block_bytes = (2 * ext * k * 4 + branches * 2 * k * tn * 2 + 2 * tm * tn * 2 + ext * k * 2
                   + 3 * branches * ext * sub * 4 + 8 * CONV_ROWS * tn * 4)
    w_spec = pl.BlockSpec((k, tn), lambda i, j: (0, j))
    cw_spec = pl.BlockSpec((taps, tn), lambda i, j: (0, j))
    cb_spec = pl.BlockSpec((1, tn), lambda i, j: (0, j))
    return pl.pallas_call(
        functools.partial(_norm_matmul_conv_kernel, branches=branches, tiles_per_seq=seq_len // tm),
        grid=(m // tm, n // tn),
        in_specs=[
            pl.BlockSpec((tm, k), lambda i, j: (i, 0)),
            pl.BlockSpec((CONV_HALO, k), lambda i, j: (jnp.maximum(i * hb - 1, 0), 0)),
            pl.BlockSpec((CONV_HALO, k), lambda i, j: (jnp.minimum((i + 1) * hb, last_hb), 0)),
            pl.BlockSpec((1, k), lambda i, j: (0, 0)),
        ] + [w_spec] * branches + [cw_spec] * branches + [cb_spec] * branches,
        out_specs=pl.BlockSpec((tm, tn), lambda i, j: (i, j)),
        out_shape=jax.ShapeDtypeStruct((m, n), BF16),
        scratch_shapes=[pltpu.VMEM((ext, k), BF16)] + [pltpu.VMEM((2, ext, sub), F32)] * branches,
        compiler_params=_params(("parallel", "arbitrary"), block_bytes),
        name="norm_matmul_conv",
    )(h, h, h, norm_w.reshape(1, k).astype(F32), *ws, *[w.astype(F32) for w in conv_ws],
      *[b.reshape(1, n).astype(F32) for b in conv_bs])


LOG2E = math.log2(math.e)


def _split3(x):
    hi = x.astype(BF16)
    r1 = x - hi.astype(F32)
    mid = r1.astype(BF16)
    lo = (r1 - mid.astype(F32)).astype(BF16)
    return hi, mid, lo


def _softplus(x):
    return jnp.maximum(x, 0.0) + jnp.log(1.0 + jnp.exp(-jnp.abs(x)))


def _ssd_kernel(z_ref, x_ref, b_ref, c_ref, dt_ref, bias_ref, alog_ref, d_ref, nw_ref, y_ref,
                s_ref, colval_ref, dts_ref, rowval_ref, tot_ref, cdec_ref, carry_ref, xd_ref, *, hpg):
    t = SSM_CHUNK
    seq_len, gch = x_ref.shape
    nchunks = seq_len // t
    period = 2 * hpg
    hd = gch // hpg
    npairs = gch // V7X_LANES
    assert hd * 2 == V7X_LANES and t == V7X_LANES and 3 * period <= V7X_LANES
    assert hd & (hd - 1) == 0 and period & (period - 1) == 0

    row = lax.broadcasted_iota(jnp.int32, (t, t), 0)
    col = lax.broadcasted_iota(jnp.int32, (t, t), 1)
    lower = row >= col
    tri = jnp.where(lower, 1.0, 0.0).astype(BF16)
    lane = col
    is_fwd = (lane & (period - 1)) < hpg
    lo_half = lane < hd
    e_row = lax.broadcasted_iota(jnp.int32, (V7X_LANES, 2 * gch), 0)
    e_col = lax.broadcasted_iota(jnp.int32, (V7X_LANES, 2 * gch), 1)
    match = (e_row & (period - 1)) == (e_col >> (hd.bit_length() - 1))
    expand2 = jnp.where(match & (e_row < 2 * period), 1.0, 0.0).astype(BF16)
    expand3 = jnp.where(match & (e_row < 3 * period), 1.0, 0.0).astype(BF16)

    def expand_cols(v, cols=slice(None)):
        hi = v.astype(BF16)
        mid = (v - hi.astype(F32)).astype(BF16)
        return jnp.dot(jnp.where(lane[:v.shape[0]] < period, hi, mid), expand2[:, cols],
                       preferred_element_type=F32)

    bias = bias_ref[...]
    a2c = -jnp.exp(alog_ref[...]) * LOG2E

    def scalars(c, carry):
        rows = pl.ds(pl.multiple_of(c * t, t), t)
        dt = _softplus(dt_ref[rows, :] + bias)
        a2 = dt * a2c
        cum3 = jnp.dot(tri, jnp.concatenate(_split3(a2), axis=1), preferred_element_type=F32)
        cum = cum3[:, :t] + cum3[:, t:2 * t] + cum3[:, 2 * t:]
        tot = cum[t - 1:t, :]
        colval = jnp.where(is_fwd, cum, tot - cum + a2)
        colval_ref[rows, :] = colval
        dts_ref[rows, :] = dt
        rowval_ref[:, rows] = (colval - jnp.log2(dt)).T[:period, :]
        tot_ref[pl.ds(c, 1), :] = tot
        return carry

    lax.fori_loop(0, nchunks, scalars, 0, unroll=4)

    cd3 = _split3(jnp.exp2(tot_ref[...]))
    lane_n = lax.broadcasted_iota(jnp.int32, (nchunks, V7X_LANES), 1)
    cd_lhs = jnp.where(lane_n < period, cd3[0], jnp.where(lane_n < 2 * period, cd3[1], cd3[2]))
    cdec_ref[...] = jnp.dot(cd_lhs, expand3, preferred_element_type=F32)

    def weighted_inputs(c, carry):
        rows = pl.ds(pl.multiple_of(c * t, t), t)
        colval = colval_ref[rows, :]
        tot = jnp.where(is_fwd[:1], colval[t - 1:t, :], colval[0:1, :])
        wexp = expand_cols(jnp.exp2(tot - colval) * dts_ref[rows, :])
        x = x_ref[rows, :].astype(F32)
        xd_ref[rows, :] = jnp.concatenate([x * wexp[:, :gch], x * wexp[:, gch:]], axis=1).astype(BF16)
        return carry

    lax.fori_loop(0, nchunks, weighted_inputs, 0, unroll=4)

    def local_states(c, carry):
        rows = pl.ds(pl.multiple_of(c * t, t), t)
        loc = lax.dot_general(b_ref[rows, :], xd_ref[rows, :], (((0,), (0,)), ((), ())),
                              preferred_element_type=F32)
        s_ref[c] = loc.astype(BF16)
        return carry

    lax.fori_loop(0, nchunks, local_states, 0, unroll=4)

    def scan(lanes, reverse):
        carry_ref[...] = jnp.zeros_like(carry_ref)

        def body(i, carry):
            c = (nchunks - 1 - i) if reverse else i
            loc = s_ref[c, :, lanes].astype(F32)
            prev = carry_ref[...]
            s_ref[c, :, lanes] = prev.astype(BF16)
            carry_ref[...] = prev * cdec_ref[pl.ds(c, 1), lanes] + loc
            return carry

        lax.fori_loop(0, nchunks, body, 0, unroll=2)

    scan(slice(0, gch), False)
    scan(slice(gch, 2 * gch), True)

    dvec = d_ref[...]
    nw = nw_ref[...]
    zero = jnp.zeros((), BF16)

    def outputs(c, carry):
        rows = pl.ds(pl.multiple_of(c * t, t), t)
        colval = colval_ref[rows, :]
        rowval = rowval_ref[:, rows]
        xb = x_ref[rows, :]
        bm = b_ref[rows, :]
        cm = c_ref[rows, :]
        cs = jnp.dot(cm, s_ref[c], preferred_element_type=F32) * expand_cols(jnp.exp2(colval))
        cb = lax.dot_general(cm, bm, (((1,), (1,)), ((), ())), preferred_element_type=F32)
        diag = jnp.sum(jnp.where(row == col, cb, 0.0), axis=-1, keepdims=True)
        coef = expand_cols(diag * dts_ref[rows, :], slice(gch, 2 * gch))
        y = cs[:, :gch] + cs[:, gch:] + xb.astype(F32) * (dvec + coef)
        ys = []
        for p in range(npairs):
            ms = []
            for q in range(2):
                j = 2 * p + q
                e = jnp.where(lower, colval[:, j:j + 1] - rowval[j:j + 1, :],
                              colval[:, hpg + j:hpg + j + 1] - rowval[hpg + j:hpg + j + 1, :])
                ms.append(cb * jnp.exp2(e))
            xp = xb[:, p * V7X_LANES:(p + 1) * V7X_LANES]
            rhs = jnp.concatenate([jnp.where(lo_half, xp, zero), jnp.where(lo_half, zero, xp)], axis=0)
            ys.append(jnp.dot(jnp.concatenate(ms, axis=1).astype(BF16), rhs, preferred_element_type=F32))
        y = y + jnp.concatenate(ys, axis=1)
        y = y * _silu(z_ref[rows, :].astype(F32))
        ms = jnp.mean(y * y, axis=-1, keepdims=True)
        y_ref[rows, :] = (y * lax.rsqrt(ms + NORM_EPS) * nw).astype(y_ref.dtype)
        return carry

    lax.fori_loop(0, nchunks, outputs, 0, unroll=4)


def _ssd(zx, xc, dt_raw, dt_bias, a_log, d_skip, norm_w, d_inner, heads):
    bsz, seq_len, _ = xc.shape
    g = SSM_GROUPS
    hpg = heads // g
    gch = d_inner // g
    state = (xc.shape[-1] - d_inner) // (2 * g)
    assert state == V7X_LANES
    nchunks = seq_len // SSM_CHUNK
    reps = V7X_LANES // (2 * hpg)

    def per_group(p, lead):
        p = p.astype(F32).reshape(lead + (2, g, hpg))
        p = jnp.moveaxis(p, -3, -2).reshape(lead + (g, 2 * hpg))
        return jnp.tile(p, (1,) * (len(lead) + 1) + (reps,))

    dt_t = jnp.moveaxis(per_group(dt_raw, (bsz, seq_len)), 2, 1)
    bias_g = per_group(dt_bias, ()).reshape(g, 1, V7X_LANES)
    alog_g = per_group(a_log, ()).reshape(g, 1, V7X_LANES)
    d_exp = jnp.repeat(d_skip.astype(F32), gch // hpg).reshape(g, 1, gch)
    nw = norm_w.astype(F32).reshape(g, 1, gch)
    xoff = d_inner // V7X_LANES
    block_bytes = (2 * 3 * seq_len * gch * 2 + 2 * 2 * seq_len * state * 2 + 2 * seq_len * V7X_LANES * 4
                   + nchunks * state * 2 * gch * 2 + 2 * seq_len * V7X_LANES * 4 + 2 * 2 * hpg * seq_len * 4
                   + nchunks * 2 * gch * 4 + state * gch * 4 + seq_len * 2 * gch * 2)
    return pl.pallas_call(
        functools.partial(_ssd_kernel, hpg=hpg),
        grid=(bsz, g),
        in_specs=[
            pl.BlockSpec((None, seq_len, gch), lambda i, j: (i, 0, j)),
            pl.BlockSpec((None, seq_len, gch), lambda i, j: (i, 0, j)),
            pl.BlockSpec((None, seq_len, state), lambda i, j: (i, 0, xoff + j)),
            pl.BlockSpec((None, seq_len, state), lambda i, j: (i, 0, xoff + g + j)),
            pl.BlockSpec((None, None, seq_len, V7X_LANES), lambda i, j: (i, j, 0, 0)),
            pl.BlockSpec((None, 1, V7X_LANES), lambda i, j: (j, 0, 0)),
            pl.BlockSpec((None, 1, V7X_LANES), lambda i, j: (j, 0, 0)),
            pl.BlockSpec((None, 1, gch), lambda i, j: (j, 0, 0)),
            pl.BlockSpec((None, 1, gch), lambda i, j: (j, 0, 0)),
        ],
        out_specs=pl.BlockSpec((None, seq_len, gch), lambda i, j: (i, 0, j)),
        out_shape=jax.ShapeDtypeStruct((bsz, seq_len, d_inner), BF16),
        scratch_shapes=[
            pltpu.VMEM((nchunks, state, 2 * gch), BF16),
            pltpu.VMEM((seq_len, V7X_LANES), F32),
            pltpu.VMEM((seq_len, V7X_LANES), F32),
            pltpu.VMEM((2 * hpg, seq_len), F32),
            pltpu.VMEM((nchunks, V7X_LANES), F32),
            pltpu.VMEM((nchunks, 2 * gch), F32),
            pltpu.VMEM((state, gch), F32),
            pltpu.VMEM((seq_len, 2 * gch), BF16),
        ],
        compiler_params=_params(("parallel", "parallel"), block_bytes),
        name="ssd_scan",
    )(zx, xc, xc, xc, dt_t, bias_g, alog_g, d_exp, nw)


QK_ROWS = 512


def _qk_prep_kernel(q_ref, k_ref, qw_ref, kw_ref, cos_ref, sin_ref, qo_ref, ko_ref, *, q_scale):
    dh = cos_ref.shape[-1]
    cos = cos_ref[...]
    sin = sin_ref[...]
    lane = lax.broadcasted_iota(jnp.int32, cos.shape, 1)
    assert dh & (dh - 1) == 0
    first = (lane & (dh // 2 - 1)) < (dh // 4)

    def prep(x, w, scale):
        x = x.astype(F32)
        ms = jnp.mean(x * x, axis=-1, keepdims=True)
        x = x * lax.rsqrt(ms + NORM_EPS) * w
        partner = jnp.where(first, pltpu.roll(x, dh - dh // 4, 1), pltpu.roll(x, dh // 4, 1))
        out = x * cos + partner * sin
        return out * scale if scale != 1.0 else out

    for h in range(q_ref.shape[-1] // dh):
        sl = slice(h * dh, (h + 1) * dh)
        qo_ref[:, sl] = prep(q_ref[:, sl], qw_ref[...], q_scale).astype(qo_ref.dtype)
    for h in range(k_ref.shape[-1] // dh):
        sl = slice(h * dh, (h + 1) * dh)
        ko_ref[:, sl] = prep(k_ref[:, sl], kw_ref[...], 1.0).astype(ko_ref.dtype)


def _qk_prep(qkv, q_norm_w, k_norm_w, cos_full, sin_signed, n_heads, q_scale):
    bsz, seq_len, _ = qkv.shape
    dh = q_norm_w.shape[0]
    qd, kd = n_heads * dh, ATT_KV_HEADS * dh
    assert qd % kd == 0
    tr = _largest_tile(seq_len, QK_ROWS, BF16_SUBLANES)
    block_bytes = 4 * tr * (qd + kd) * 2 + 16 * tr * dh * 4
    return pl.pallas_call(
        functools.partial(_qk_prep_kernel, q_scale=q_scale),
        grid=(bsz, seq_len // tr),
        in_specs=[
            pl.BlockSpec((None, tr, qd), lambda i, j: (i, j, 0)),
            pl.BlockSpec((None, tr, kd), lambda i, j: (i, j, qd // kd)),
            pl.BlockSpec((1, dh), lambda i, j: (0, 0)),
            pl.BlockSpec((1, dh), lambda i, j: (0, 0)),
            pl.BlockSpec((tr, dh), lambda i, j: (j, 0)),
            pl.BlockSpec((tr, dh), lambda i, j: (j, 0)),
        ],
        out_specs=[
            pl.BlockSpec((None, tr, qd), lambda i, j: (i, j, 0)),
            pl.BlockSpec((None, tr, kd), lambda i, j: (i, j, 0)),
        ],
        out_shape=[jax.ShapeDtypeStruct((bsz, seq_len, qd), BF16), jax.ShapeDtypeStruct((bsz, seq_len, kd), BF16)],
        compiler_params=_params(("parallel", "parallel"), block_bytes),
        name="qk_prep",
    )(qkv, qkv, q_norm_w.reshape(1, dh).astype(F32), k_norm_w.reshape(1, dh).astype(F32), cos_full, sin_signed)


ATT_Q_ROWS = 512


def _attention_kernel(q_ref, k_ref, v_ref, o_ref):
    tq, width = q_ref.shape
    dh = k_ref.shape[-1]
    rep = width // dh
    q = q_ref[...]
    qs = jnp.concatenate([q[:, r * dh:(r + 1) * dh] for r in range(rep)], axis=0)
    s = lax.dot_general(k_ref[...], qs, (((1,), (1,)), ((), ())), preferred_element_type=F32)
    m = jnp.max(s, axis=0, keepdims=True)
    p = jnp.exp2(s - m)
    denom = jnp.sum(p, axis=0, keepdims=True)
    acc = lax.dot_general(v_ref[...], p.astype(v_ref.dtype), (((0,), (0,)), ((), ())),
                          preferred_element_type=F32)
    out = (acc * (1.0 / denom)).T
    for r in range(rep):
        o_ref[:, r * dh:(r + 1) * dh] = out[r * tq:(r + 1) * tq, :].astype(o_ref.dtype)


def _attention(q, k, qkv, n_heads):
    bsz, seq_len, qd = q.shape
    dh = qd // n_heads
    rep = n_heads // ATT_KV_HEADS
    width = rep * dh
    voff = (n_heads + ATT_KV_HEADS)
    tq = ATT_Q_ROWS
    block_bytes = (4 * tq * width * 2 + 4 * seq_len * dh * 2 + seq_len * rep * tq * (4 + 4 + 2)
                   + 4 * rep * tq * dh * 4)
    return pl.pallas_call(
        _attention_kernel,
        grid=(bsz, ATT_KV_HEADS, seq_len // tq),
        in_specs=[
            pl.BlockSpec((None, tq, width), lambda b, h, i: (b, i, h)),
            pl.BlockSpec((None, seq_len, dh), lambda b, h, i: (b, 0, h)),
            pl.BlockSpec((None, seq_len, dh), lambda b, h, i: (b, 0, voff + h)),
        ],
        out_specs=pl.BlockSpec((None, tq, width), lambda b, h, i: (b, i, h)),
        out_shape=jax.ShapeDtypeStruct((bsz, seq_len, qd), BF16),
        compiler_params=_params(("parallel", "parallel", "parallel"), block_bytes),
        name="attention",
    )(q, k, qkv)


def _rope_tables(seq_len, dh):
    quarter = dh // 4
    tpos = jnp.arange(seq_len)
    pos = jnp.stack([tpos // GRID_W, tpos % GRID_W], axis=-1).astype(F32)
    inv_freq = ROPE_THETA ** (-jnp.arange(quarter, dtype=F32) / quarter)
    ang = pos[:, :, None] * inv_freq
    cos, sin = jnp.cos(ang), jnp.sin(ang)
    cos_full = jnp.concatenate([cos, cos], axis=-1).reshape(seq_len, dh)
    sin_signed = jnp.concatenate([-sin, sin], axis=-1).reshape(seq_len, dh)
    return cos_full, sin_signed


def _ssd_layer(h, norm_w, w_in, conv_w, conv_b, dt_bias, a_log, d_skip, gn_w, w_out):
    bsz, seq_len, d_model = h.shape
    d_inner = gn_w.shape[0]
    heads = a_log.shape[-1]
    conv_dim = conv_w.shape[-1]
    h2 = h.reshape(bsz * seq_len, d_model)
    w_z = w_in[:, :d_inner].astype(BF16)
    w_xbc = w_in[:, d_inner:d_inner + conv_dim].astype(BF16)
    w_dt = w_in[:, d_inner + conv_dim:].astype(BF16)
    z, dt_raw = _norm_matmul(h2, norm_w, w_z, BF16, w_side=w_dt)
    z = z.reshape(bsz, seq_len, d_inner)
    dt_raw = dt_raw.reshape(bsz, seq_len, 2 * heads)
    xc = _norm_matmul_conv(h2, norm_w, [w_xbc], [conv_w], [conv_b], seq_len,
                           sub_cols=CONV_SUB_COLS).reshape(bsz, seq_len, conv_dim)
    y = _ssd(z, xc, dt_raw, dt_bias, a_log, d_skip, gn_w, d_inner, heads)
    out = _matmul_residual(y.reshape(bsz * seq_len, d_inner), w_out.astype(BF16), h2)
    return out.reshape(bsz, seq_len, d_model)


def _gqa_layer(h, norm_w, w_qkv, q_norm_w, k_norm_w, w_o, tables):
    bsz, seq_len, d_model = h.shape
    dh = q_norm_w.shape[0]
    n_heads = w_o.shape[0] // dh
    h2 = h.reshape(bsz * seq_len, d_model)
    qkv = _norm_matmul(h2, norm_w, w_qkv.astype(BF16), BF16).reshape(bsz, seq_len, -1)
    q, k = _qk_prep(qkv, q_norm_w, k_norm_w, tables[0], tables[1], n_heads, LOG2E * dh ** -0.5)
    o = _attention(q, k, qkv, n_heads)
    out = _matmul_residual(o.reshape(bsz * seq_len, n_heads * dh), w_o.astype(BF16), h2)
    return out.reshape(bsz, seq_len, d_model)


def _ffn_layer(h, norm_w, w_up, conv_w, conv_b, w_down):
    bsz, seq_len, d_model = h.shape
    h2 = h.reshape(bsz * seq_len, d_model)
    f = w_down.shape[0]
    gated = _norm_matmul_conv(h2, norm_w, [w_up[:, :f].astype(BF16), w_up[:, f:].astype(BF16)],
                              [conv_w[:, :f], conv_w[:, f:]], [conv_b[:f], conv_b[f:]], seq_len)
    out = _matmul_residual(gated, w_down.astype(BF16), h2)
    return out.reshape(bsz, seq_len, d_model)


def kernel(x, norm_mix_w, norm_ffn_w, ssm_w_in, ssm_conv_w, ssm_conv_b, ssm_dt_bias, ssm_a_log, ssm_d, ssm_norm_w, ssm_w_out, att_w_qkv, att_q_norm_w, att_k_norm_w, att_w_o, ffn_w_up, ffn_conv_w, ffn_conv_b, ffn_w_down):
    depth = norm_mix_w.shape[0]
    tables = _rope_tables(x.shape[1], att_q_norm_w.shape[-1])
    h = x
    for i in range(depth):
        j = i // N_MIXERS
        if i % N_MIXERS == 0:
            h = _ssd_layer(h, norm_mix_w[i], ssm_w_in[j], ssm_conv_w[j], ssm_conv_b[j], ssm_dt_bias[j],
                           ssm_a_log[j], ssm_d[j], ssm_norm_w[j], ssm_w_out[j])
        else:
            h = _gqa_layer(h, norm_mix_w[i], att_w_qkv[j], att_q_norm_w[j], att_k_norm_w[j], att_w_o[j], tables)
        h = _ffn_layer(h, norm_ffn_w[i], ffn_w_up[i], ffn_conv_w[i], ffn_conv_b[i], ffn_w_down[i])
    return h
```

```python
import functools
import math

import jax
import jax.numpy as jnp
from jax import lax
from jax.experimental import pallas as pl
from jax.experimental.pallas import tpu as pltpu

NORM_EPS = 1e-6
N_MIXERS = 2
SSM_GROUPS = 8
SSM_CHUNK = 128
ATT_KV_HEADS = 4
GRID_W = 64
ROPE_THETA = 10000.0

V7X_VMEM_BYTES = 64 * 1024 * 1024
V7X_LANES = 128
BF16_SUBLANES = 16

F32 = jnp.float32
BF16 = jnp.bfloat16


def _vmem_limit(block_bytes):
    return int(min(V7X_VMEM_BYTES - (6 << 20), max(32 << 20, 2 * block_bytes)))


def _params(semantics, block_bytes, flags=None):
    return pltpu.CompilerParams(dimension_semantics=semantics, vmem_limit_bytes=_vmem_limit(block_bytes), flags=flags)


def _largest_tile(n, cap, quantum):
    best = n if n < quantum else None
    t = quantum
    while t <= min(n, cap):
        if n % t == 0:
            best = t
        t += quantum
    assert best is not None, (n, cap, quantum)
    return best


NORM_ROWS = 128


def _norm_matmul_kernel(*refs, side):
    if side:
        h_ref, nw_ref, w_ref, ws_ref, o_ref, os_ref, a_ref = refs
    else:
        h_ref, nw_ref, w_ref, o_ref, a_ref = refs

    @pl.when(pl.program_id(1) == 0)
    def _():
        def body(r, carry):
            rows = pl.ds(pl.multiple_of(r * NORM_ROWS, NORM_ROWS), NORM_ROWS)
            x = h_ref[rows, :]
            ms = jnp.mean(x * x, axis=-1, keepdims=True)
            a_ref[rows, :] = (x * lax.rsqrt(ms + NORM_EPS) * nw_ref[...]).astype(a_ref.dtype)
            return carry

        lax.fori_loop(0, h_ref.shape[0] // NORM_ROWS, body, 0)
        if side:
            os_ref[...] = jnp.dot(a_ref[...], ws_ref[...], preferred_element_type=F32)

    o_ref[...] = jnp.dot(a_ref[...], w_ref[...], preferred_element_type=F32).astype(o_ref.dtype)


def _norm_matmul(h, norm_w, w, out_dtype, w_side=None):
    m, k = h.shape
    n = w.shape[1]
    tm = _largest_tile(m, 1024, NORM_ROWS)
    tn = _largest_tile(n, 1024, V7X_LANES)
    ob = jnp.dtype(out_dtype).itemsize
    block_bytes = 2 * tm * k * 4 + 2 * k * tn * 2 + 2 * tm * tn * ob + tm * k * 2 + tm * tn * 4
    in_specs = [
        pl.BlockSpec((tm, k), lambda i, j: (i, 0)),
        pl.BlockSpec((1, k), lambda i, j: (0, 0)),
        pl.BlockSpec((k, tn), lambda i, j: (0, j)),
    ]
    out_specs = [pl.BlockSpec((tm, tn), lambda i, j: (i, j))]
    out_shape = [jax.ShapeDtypeStruct((m, n), out_dtype)]
    operands = [h, norm_w.reshape(1, k).astype(F32), w]
    if w_side is not None:
        ns = w_side.shape[1]
        in_specs.append(pl.BlockSpec((k, ns), lambda i, j: (0, 0)))
        out_specs.append(pl.BlockSpec((tm, ns), lambda i, j: (i, 0)))
        out_shape.append(jax.ShapeDtypeStruct((m, ns), F32))
        operands.append(w_side)
        block_bytes += 2 * k * ns * 2 + 3 * tm * ns * 4
    outs = pl.pallas_call(
        functools.partial(_norm_matmul_kernel, side=w_side is not None),
        grid=(m // tm, n // tn),
        in_specs=in_specs,
        out_specs=out_specs,
        out_shape=out_shape,
        scratch_shapes=[pltpu.VMEM((tm, k), BF16)],
        compiler_params=_params(("parallel", "arbitrary"), block_bytes),
        name="norm_matmul",
    )(*operands)
    return outs if w_side is not None else outs[0]


WEIGHT_RESIDENT_BYTES = 8 * 1024 * 1024


def _matmul_residual_kernel(a_ref, w_ref, r_ref, o_ref):
    o_ref[...] = r_ref[...] + jnp.dot(a_ref[...], w_ref[...], preferred_element_type=F32)


def _matmul_residual(a, w, res):
    m, k = a.shape
    n = w.shape[1]
    if k * n * 2 <= WEIGHT_RESIDENT_BYTES:
        tm, tn, cols_outer = _largest_tile(m, 512, V7X_LANES), n, True
    elif k < 4096:
        tm, tn, cols_outer = _largest_tile(m, 1024, V7X_LANES), _largest_tile(n, 512, V7X_LANES), False
    else:
        tm, tn, cols_outer = _largest_tile(m, 512, V7X_LANES), _largest_tile(n, 1024, V7X_LANES), True
    block_bytes = 2 * tm * k * 2 + 2 * k * tn * 2 + 4 * tm * tn * 4 + tm * tn * 4
    if cols_outer:
        grid = (n // tn, m // tm)
        row_col = lambda j, i: (i, j)
    else:
        grid = (m // tm, n // tn)
        row_col = lambda i, j: (i, j)
    return pl.pallas_call(
        _matmul_residual_kernel,
        grid=grid,
        in_specs=[
            pl.BlockSpec((tm, k), lambda *g: (row_col(*g)[0], 0)),
            pl.BlockSpec((k, tn), lambda *g: (0, row_col(*g)[1])),
            pl.BlockSpec((tm, tn), lambda *g: row_col(*g)),
        ],
        out_specs=pl.BlockSpec((tm, tn), lambda *g: row_col(*g)),
        out_shape=jax.ShapeDtypeStruct((m, n), F32),
        compiler_params=_params(("parallel", "parallel"), block_bytes),
        name="matmul_residual",
    )(a, w, res)


CONV_ROWS = 256
CONV_HALO = BF16_SUBLANES
CONV_SUB_COLS = 256


def _silu(x):
    half = 0.5 * x
    return half + half * jnp.tanh(half)


def _rmsnorm_rows(x, nw):
    ms = jnp.mean(x * x, axis=-1, keepdims=True)
    return (x * lax.rsqrt(ms + NORM_EPS) * nw).astype(BF16)


def _norm_matmul_conv_kernel(*refs, branches, tiles_per_seq):
    h_ref, hp_ref, hn_ref, nw_ref = refs[:4]
    w_refs = refs[4:4 + branches]
    cw_refs = refs[4 + branches:4 + 2 * branches]
    cb_refs = refs[4 + 2 * branches:4 + 3 * branches]
    o_ref, a_ref = refs[4 + 3 * branches], refs[5 + 3 * branches]
    u_refs = refs[6 + 3 * branches:]
    i, j = pl.program_id(0), pl.program_id(1)
    tm = h_ref.shape[0]
    halo = CONV_HALO

    @pl.when(j == 0)
    def _():
        nw = nw_ref[...]
        starts_seq = (i % tiles_per_seq) == 0
        ends_seq = (i % tiles_per_seq) == tiles_per_seq - 1
        a_ref[0:halo, :] = jnp.where(starts_seq, 0.0, _rmsnorm_rows(hp_ref[...], nw).astype(F32)).astype(BF16)
        a_ref[halo + tm:, :] = jnp.where(ends_seq, 0.0, _rmsnorm_rows(hn_ref[...], nw).astype(F32)).astype(BF16)

        def body(r, carry):
            rows = pl.multiple_of(r * NORM_ROWS, NORM_ROWS)
            a_ref[pl.ds(halo + rows, NORM_ROWS), :] = _rmsnorm_rows(h_ref[pl.ds(rows, NORM_ROWS), :], nw)
            return carry

        lax.fori_loop(0, tm // NORM_ROWS, body, 0)

    taps = cw_refs[0].shape[0]
    pad = taps // 2
    tn = o_ref.shape[1]
    sub = u_refs[0].shape[-1]
    for s, n0 in enumerate(range(0, tn, sub)):
        cols = slice(n0, n0 + sub)
        slot = s % 2
        for w_ref, u_ref in zip(w_refs, u_refs):
            u_ref[slot] = jnp.dot(a_ref[...], w_ref[:, cols], preferred_element_type=F32)
        for r0 in range(0, tm, CONV_ROWS):
            convs = []
            for u_ref, cw_ref, cb_ref in zip(u_refs, cw_refs, cb_refs):
                acc = cb_ref[:, cols]
                for k in range(taps):
                    start = halo + r0 - pad + k
                    acc = acc + u_ref[slot, start:start + CONV_ROWS, :] * cw_ref[k:k + 1, cols]
                convs.append(acc)
            out = _silu(convs[0]) * convs[1] if branches == 2 else _silu(convs[0])
            o_ref[r0:r0 + CONV_ROWS, cols] = out.astype(o_ref.dtype)


def _norm_matmul_conv(h, norm_w, ws, conv_ws, conv_bs, seq_len, sub_cols=None):
    m, k = h.shape
    n = ws[0].shape[1]
    branches = len(ws)
    tm = _largest_tile(seq_len, 1024, CONV_ROWS)
    tn = _largest_tile(n, 1024 // branches, V7X_LANES)
    sub = tn if sub_cols is None else _largest_tile(tn, sub_cols, V7X_LANES)
    hb = tm // CONV_HALO
    last_hb = m // CONV_HALO - 1
    taps = conv_ws[0].shape[0]
    ext = tm + 2 * CONV_HALO
    block_bytes = (2 * ext * k * 4 + branches * 2 * k * tn * 2 + 2 * tm * tn * 2 + ext * k * 2
                   + 3 * branches * ext * sub * 4 + 8 * CONV_ROWS * tn * 4)
    w_spec = pl.BlockSpec((k, tn), lambda i, j: (0, j))
    cw_spec = pl.BlockSpec((taps, tn), lambda i, j: (0, j))
    cb_spec = pl.BlockSpec((1, tn), lambda i, j: (0, j))
    return pl.pallas_call(
        functools.partial(_norm_matmul_conv_kernel, branches=branches, tiles_per_seq=seq_len // tm),
        grid=(m // tm, n // tn),
        in_specs=[
            pl.BlockSpec((tm, k), lambda i, j: (i, 0)),
            pl.BlockSpec((CONV_HALO, k), lambda i, j: (jnp.maximum(i * hb - 1, 0), 0)),
            pl.BlockSpec((CONV_HALO, k), lambda i, j: (jnp.minimum((i + 1) * hb, last_hb), 0)),
            pl.BlockSpec((1, k), lambda i, j: (0, 0)),
        ] + [w_spec] * branches + [cw_spec] * branches + [cb_spec] * branches,
        out_specs=pl.BlockSpec((tm, tn), lambda i, j: (i, j)),
        out_shape=jax.ShapeDtypeStruct((m, n), BF16),
        scratch_shapes=[pltpu.VMEM((ext, k), BF16)] + [pltpu.VMEM((2, ext, sub), F32)] * branches,
        compiler_params=_params(("parallel", "arbitrary"), block_bytes),
        name="norm_matmul_conv",
    )(h, h, h, norm_w.reshape(1, k).astype(F32), *ws, *[w.astype(F32) for w in conv_ws],
      *[b.reshape(1, n).astype(F32) for b in conv_bs])


LOG2E = math.log2(math.e)


def _split3(x):
    hi = x.astype(BF16)
    r1 = x - hi.astype(F32)
    mid = r1.astype(BF16)
    lo = (r1 - mid.astype(F32)).astype(BF16)
    return hi, mid, lo


def _softplus(x):
    return jnp.maximum(x, 0.0) + jnp.log(1.0 + jnp.exp(-jnp.abs(x)))


def _ssd_kernel(z_ref, x_ref, b_ref, c_ref, dt_ref, bias_ref, alog_ref, d_ref, nw_ref, y_ref,
                s_ref, colval_ref, dts_ref, rowval_ref, tot_ref, cdec_ref, carry_ref, xd_ref, *, hpg):
    t = SSM_CHUNK
    seq_len, gch = x_ref.shape
    nchunks = seq_len // t
    period = 2 * hpg
    hd = gch // hpg
    npairs = gch // V7X_LANES
    assert hd * 2 == V7X_LANES and t == V7X_LANES and 3 * period <= V7X_LANES
    assert hd & (hd - 1) == 0 and period & (period - 1) == 0

    row = lax.broadcasted_iota(jnp.int32, (t, t), 0)
    col = lax.broadcasted_iota(jnp.int32, (t, t), 1)
    lower = row >= col
    tri = jnp.where(lower, 1.0, 0.0).astype(BF16)
    lane = col
    is_fwd = (lane & (period - 1)) < hpg
    lo_half = lane < hd
    e_row = lax.broadcasted_iota(jnp.int32, (V7X_LANES, 2 * gch), 0)
    e_col = lax.broadcasted_iota(jnp.int32, (V7X_LANES, 2 * gch), 1)
    match = (e_row & (period - 1)) == (e_col >> (hd.bit_length() - 1))
    expand2 = jnp.where(match & (e_row < 2 * period), 1.0, 0.0).astype(BF16)
    expand3 = jnp.where(match & (e_row < 3 * period), 1.0, 0.0).astype(BF16)

    def expand_cols(v, cols=slice(None)):
        hi = v.astype(BF16)
        mid = (v - hi.astype(F32)).astype(BF16)
        return jnp.dot(jnp.where(lane[:v.shape[0]] < period, hi, mid), expand2[:, cols],
                       preferred_element_type=F32)

    bias = bias_ref[...]
    a2c = -jnp.exp(alog_ref[...]) * LOG2E

    def scalars(c, carry):
        rows = pl.ds(pl.multiple_of(c * t, t), t)
        dt = _softplus(dt_ref[rows, :] + bias)
        a2 = dt * a2c
        cum3 = jnp.dot(tri, jnp.concatenate(_split3(a2), axis=1), preferred_element_type=F32)
        cum = cum3[:, :t] + cum3[:, t:2 * t] + cum3[:, 2 * t:]
        tot = cum[t - 1:t, :]
        colval = jnp.where(is_fwd, cum, tot - cum + a2)
        colval_ref[rows, :] = colval
        dts_ref[rows, :] = dt
        rowval_ref[:, rows] = (colval - jnp.log2(dt)).T[:period, :]
        tot_ref[pl.ds(c, 1), :] = tot
        return carry

    lax.fori_loop(0, nchunks, scalars, 0, unroll=4)

    cd3 = _split3(jnp.exp2(tot_ref[...]))
    lane_n = lax.broadcasted_iota(jnp.int32, (nchunks, V7X_LANES), 1)
    cd_lhs = jnp.where(lane_n < period, cd3[0], jnp.where(lane_n < 2 * period, cd3[1], cd3[2]))
    cdec_ref[...] = jnp.dot(cd_lhs, expand3, preferred_element_type=F32)

    def weighted_inputs(c, carry):
        rows = pl.ds(pl.multiple_of(c * t, t), t)
        colval = colval_ref[rows, :]
        tot = jnp.where(is_fwd[:1], colval[t - 1:t, :], colval[0:1, :])
        wexp = expand_cols(jnp.exp2(tot - colval) * dts_ref[rows, :])
        x = x_ref[rows, :].astype(F32)
        xd_ref[rows, :] = jnp.concatenate([x * wexp[:, :gch], x * wexp[:, gch:]], axis=1).astype(BF16)
        return carry

    lax.fori_loop(0, nchunks, weighted_inputs, 0, unroll=4)

    def local_states(c, carry):
        rows = pl.ds(pl.multiple_of(c * t, t), t)
        loc = lax.dot_general(b_ref[rows, :], xd_ref[rows, :], (((0,), (0,)), ((), ())),
                              preferred_element_type=F32)
        s_ref[c] = loc.astype(BF16)
        return carry

    lax.fori_loop(0, nchunks, local_states, 0, unroll=4)

    def scan(lanes, reverse):
        carry_ref[...] = jnp.zeros_like(carry_ref)

        def body(i, carry):
            c = (nchunks - 1 - i) if reverse else i
            loc = s_ref[c, :, lanes].astype(F32)
            prev = carry_ref[...]
            s_ref[c, :, lanes] = prev.astype(BF16)
            carry_ref[...] = prev * cdec_ref[pl.ds(c, 1), lanes] + loc
            return carry

        lax.fori_loop(0, nchunks, body, 0, unroll=2)

    scan(slice(0, gch), False)
    scan(slice(gch, 2 * gch), True)

    dvec = d_ref[...]
    nw = nw_ref[...]
    zero = jnp.zeros((), BF16)

    def outputs(c, carry):
        rows = pl.ds(pl.multiple_of(c * t, t), t)
        colval = colval_ref[rows, :]
        rowval = rowval_ref[:, rows]
        xb = x_ref[rows, :]
        bm = b_ref[rows, :]
        cm = c_ref[rows, :]
        cs = jnp.dot(cm, s_ref[c], preferred_element_type=F32) * expand_cols(jnp.exp2(colval))
        cb = lax.dot_general(cm, bm, (((1,), (1,)), ((), ())), preferred_element_type=F32)
        diag = jnp.sum(jnp.where(row == col, cb, 0.0), axis=-1, keepdims=True)
        coef = expand_cols(diag * dts_ref[rows, :], slice(gch, 2 * gch))
        y = cs[:, :gch] + cs[:, gch:] + xb.astype(F32) * (dvec + coef)
        ys = []
        for p in range(npairs):
            ms = []
            for q in range(2):
                j = 2 * p + q
                e = jnp.where(lower, colval[:, j:j + 1] - rowval[j:j + 1, :],
                              colval[:, hpg + j:hpg + j + 1] - rowval[hpg + j:hpg + j + 1, :])
                ms.append(cb * jnp.exp2(e))
            xp = xb[:, p * V7X_LANES:(p + 1) * V7X_LANES]
            rhs = jnp.concatenate([jnp.where(lo_half, xp, zero), jnp.where(lo_half, zero, xp)], axis=0)
            ys.append(jnp.dot(jnp.concatenate(ms, axis=1).astype(BF16), rhs, preferred_element_type=F32))
        y = y + jnp.concatenate(ys, axis=1)
        y = y * _silu(z_ref[rows, :].astype(F32))
        ms = jnp.mean(y * y, axis=-1, keepdims=True)
        y_ref[rows, :] = (y * lax.rsqrt(ms + NORM_EPS) * nw).astype(y_ref.dtype)
        return carry

    lax.fori_loop(0, nchunks, outputs, 0, unroll=8)


def _ssd(zx, xc, dt_raw, dt_bias, a_log, d_skip, norm_w, d_inner, heads):
    bsz, seq_len, _ = xc.shape
    g = SSM_GROUPS
    hpg = heads // g
    gch = d_inner // g
    state = (xc.shape[-1] - d_inner) // (2 * g)
    assert state == V7X_LANES
    nchunks = seq_len // SSM_CHUNK
    reps = V7X_LANES // (2 * hpg)

    def per_group(p, lead):
        p = p.astype(F32).reshape(lead + (2, g, hpg))
        p = jnp.moveaxis(p, -3, -2).reshape(lead + (g, 2 * hpg))
        return jnp.tile(p, (1,) * (len(lead) + 1) + (reps,))

    dt_t = jnp.moveaxis(per_group(dt_raw, (bsz, seq_len)), 2, 1)
    bias_g = per_group(dt_bias, ()).reshape(g, 1, V7X_LANES)
    alog_g = per_group(a_log, ()).reshape(g, 1, V7X_LANES)
    d_exp = jnp.repeat(d_skip.astype(F32), gch // hpg).reshape(g, 1, gch)
    nw = norm_w.astype(F32).reshape(g, 1, gch)
    xoff = d_inner // V7X_LANES
    block_bytes = (2 * 3 * seq_len * gch * 2 + 2 * 2 * seq_len * state * 2 + 2 * seq_len * V7X_LANES * 4
                   + nchunks * state * 2 * gch * 2 + 2 * seq_len * V7X_LANES * 4 + 2 * 2 * hpg * seq_len * 4
                   + nchunks * 2 * gch * 4 + state * gch * 4 + seq_len * 2 * gch * 2)
    return pl.pallas_call(
        functools.partial(_ssd_kernel, hpg=hpg),
        grid=(bsz, g),
        in_specs=[
            pl.BlockSpec((None, seq_len, gch), lambda i, j: (i, 0, j)),
            pl.BlockSpec((None, seq_len, gch), lambda i, j: (i, 0, j)),
            pl.BlockSpec((None, seq_len, state), lambda i, j: (i, 0, xoff + j)),
            pl.BlockSpec((None, seq_len, state), lambda i, j: (i, 0, xoff + g + j)),
            pl.BlockSpec((None, None, seq_len, V7X_LANES), lambda i, j: (i, j, 0, 0)),
            pl.BlockSpec((None, 1, V7X_LANES), lambda i, j: (j, 0, 0)),
            pl.BlockSpec((None, 1, V7X_LANES), lambda i, j: (j, 0, 0)),
            pl.BlockSpec((None, 1, gch), lambda i, j: (j, 0, 0)),
            pl.BlockSpec((None, 1, gch), lambda i, j: (j, 0, 0)),
        ],
        out_specs=pl.BlockSpec((None, seq_len, gch), lambda i, j: (i, 0, j)),
        out_shape=jax.ShapeDtypeStruct((bsz, seq_len, d_inner), BF16),
        scratch_shapes=[
            pltpu.VMEM((nchunks, state, 2 * gch), BF16),
            pltpu.VMEM((seq_len, V7X_LANES), F32),
            pltpu.VMEM((seq_len, V7X_LANES), F32),
            pltpu.VMEM((2 * hpg, seq_len), F32),
            pltpu.VMEM((nchunks, V7X_LANES), F32),
            pltpu.VMEM((nchunks, 2 * gch), F32),
            pltpu.VMEM((state, gch), F32),
            pltpu.VMEM((seq_len, 2 * gch), BF16),
        ],
        compiler_params=_params(("parallel", "parallel"), block_bytes),
        name="ssd_scan",
    )(zx, xc, xc, xc, dt_t, bias_g, alog_g, d_exp, nw)


QK_ROWS = 512


def _qk_prep_kernel(q_ref, k_ref, qw_ref, kw_ref, cos_ref, sin_ref, qo_ref, ko_ref, *, q_scale):
    dh = cos_ref.shape[-1]
    cos = cos_ref[...]
    sin = sin_ref[...]
    lane = lax.broadcasted_iota(jnp.int32, cos.shape, 1)
    assert dh & (dh - 1) == 0
    first = (lane & (dh // 2 - 1)) < (dh // 4)

    def prep(x, w, scale):
        x = x.astype(F32)
        ms = jnp.mean(x * x, axis=-1, keepdims=True)
        x = x * lax.rsqrt(ms + NORM_EPS) * w
        partner = jnp.where(first, pltpu.roll(x, dh - dh // 4, 1), pltpu.roll(x, dh // 4, 1))
        out = x * cos + partner * sin
        return out * scale if scale != 1.0 else out

    for h in range(q_ref.shape[-1] // dh):
        sl = slice(h * dh, (h + 1) * dh)
        qo_ref[:, sl] = prep(q_ref[:, sl], qw_ref[...], q_scale).astype(qo_ref.dtype)
    for h in range(k_ref.shape[-1] // dh):
        sl = slice(h * dh, (h + 1) * dh)
        ko_ref[:, sl] = prep(k_ref[:, sl], kw_ref[...], 1.0).astype(ko_ref.dtype)


def _qk_prep(qkv, q_norm_w, k_norm_w, cos_full, sin_signed, n_heads, q_scale):
    bsz, seq_len, _ = qkv.shape
    dh = q_norm_w.shape[0]
    qd, kd = n_heads * dh, ATT_KV_HEADS * dh
    assert qd % kd == 0
    tr = _largest_tile(seq_len, QK_ROWS, BF16_SUBLANES)
    block_bytes = 4 * tr * (qd + kd) * 2 + 16 * tr * dh * 4
    return pl.pallas_call(
        functools.partial(_qk_prep_kernel, q_scale=q_scale),
        grid=(bsz, seq_len // tr),
        in_specs=[
            pl.BlockSpec((None, tr, qd), lambda i, j: (i, j, 0)),
            pl.BlockSpec((None, tr, kd), lambda i, j: (i, j, qd // kd)),
            pl.BlockSpec((1, dh), lambda i, j: (0, 0)),
            pl.BlockSpec((1, dh), lambda i, j: (0, 0)),
            pl.BlockSpec((tr, dh), lambda i, j: (j, 0)),
            pl.BlockSpec((tr, dh), lambda i, j: (j, 0)),
        ],
        out_specs=[
            pl.BlockSpec((None, tr, qd), lambda i, j: (i, j, 0)),
            pl.BlockSpec((None, tr, kd), lambda i, j: (i, j, 0)),
        ],
        out_shape=[jax.ShapeDtypeStruct((bsz, seq_len, qd), BF16), jax.ShapeDtypeStruct((bsz, seq_len, kd), BF16)],
        compiler_params=_params(("parallel", "parallel"), block_bytes),
        name="qk_prep",
    )(qkv, qkv, q_norm_w.reshape(1, dh).astype(F32), k_norm_w.reshape(1, dh).astype(F32), cos_full, sin_signed)


ATT_Q_ROWS = 512


def _attention_kernel(q_ref, k_ref, v_ref, o_ref):
    tq, width = q_ref.shape
    dh = k_ref.shape[-1]
    rep = width // dh
    q = q_ref[...]
    qs = jnp.concatenate([q[:, r * dh:(r + 1) * dh] for r in range(rep)], axis=0)
    s = lax.dot_general(k_ref[...], qs, (((1,), (1,)), ((), ())), preferred_element_type=F32)
    m = jnp.max(s, axis=0, keepdims=True)
    p = jnp.exp2(s - m)
    denom = jnp.sum(p, axis=0, keepdims=True)
    acc = lax.dot_general(v_ref[...], p.astype(v_ref.dtype), (((0,), (0,)), ((), ())),
                          preferred_element_type=F32)
    out = (acc * (1.0 / denom)).T
    for r in range(rep):
        o_ref[:, r * dh:(r + 1) * dh] = out[r * tq:(r + 1) * tq, :].astype(o_ref.dtype)


def _attention(q, k, qkv, n_heads):
    bsz, seq_len, qd = q.shape
    dh = qd // n_heads
    rep = n_heads // ATT_KV_HEADS
    width = rep * dh
    voff = (n_heads + ATT_KV_HEADS)
    tq = min(ATT_Q_ROWS, seq_len)
    assert seq_len % tq == 0
    block_bytes = (4 * tq * width * 2 + 4 * seq_len * dh * 2 + seq_len * rep * tq * (4 + 4 + 2)
                   + 4 * rep * tq * dh * 4)
    return pl.pallas_call(
        _attention_kernel,
        grid=(bsz, ATT_KV_HEADS, seq_len // tq),
        in_specs=[
            pl.BlockSpec((None, tq, width), lambda b, h, i: (b, i, h)),
            pl.BlockSpec((None, seq_len, dh), lambda b, h, i: (b, 0, h)),
            pl.BlockSpec((None, seq_len, dh), lambda b, h, i: (b, 0, voff + h)),
        ],
        out_specs=pl.BlockSpec((None, tq, width), lambda b, h, i: (b, i, h)),
        out_shape=jax.ShapeDtypeStruct((bsz, seq_len, qd), BF16),
        compiler_params=_params(("parallel", "parallel", "parallel"), block_bytes),
        name="attention",
    )(q, k, qkv)


def _rope_tables(seq_len, dh):
    quarter = dh // 4
    tpos = jnp.arange(seq_len)
    pos = jnp.stack([tpos // GRID_W, tpos % GRID_W], axis=-1).astype(F32)
    inv_freq = ROPE_THETA ** (-jnp.arange(quarter, dtype=F32) / quarter)
    ang = pos[:, :, None] * inv_freq
    cos, sin = jnp.cos(ang), jnp.sin(ang)
    cos_full = jnp.concatenate([cos, cos], axis=-1).reshape(seq_len, dh)
    sin_signed = jnp.concatenate([-sin, sin], axis=-1).reshape(seq_len, dh)
    return cos_full, sin_signed


def _ssd_layer(h, norm_w, w_in, conv_w, conv_b, dt_bias, a_log, d_skip, gn_w, w_out):
    bsz, seq_len, d_model = h.shape
    d_inner = gn_w.shape[0]
    heads = a_log.shape[-1]
    conv_dim = conv_w.shape[-1]
    h2 = h.reshape(bsz * seq_len, d_model)
    w_z = w_in[:, :d_inner].astype(BF16)
    w_xbc = w_in[:, d_inner:d_inner + conv_dim].astype(BF16)
    w_dt = w_in[:, d_inner + conv_dim:].astype(BF16)
    z, dt_raw = _norm_matmul(h2, norm_w, w_z, BF16, w_side=w_dt)
    z = z.reshape(bsz, seq_len, d_inner)
    dt_raw = dt_raw.reshape(bsz, seq_len, 2 * heads)
    xc = _norm_matmul_conv(h2, norm_w, [w_xbc], [conv_w], [conv_b], seq_len,
                           sub_cols=CONV_SUB_COLS).reshape(bsz, seq_len, conv_dim)
    y = _ssd(z, xc, dt_raw, dt_bias, a_log, d_skip, gn_w, d_inner, heads)
    out = _matmul_residual(y.reshape(bsz * seq_len, d_inner), w_out.astype(BF16), h2)
    return out.reshape(bsz, seq_len, d_model)


def _gqa_layer(h, norm_w, w_qkv, q_norm_w, k_norm_w, w_o, tables):
    bsz, seq_len, d_model = h.shape
    dh = q_norm_w.shape[0]
    n_heads = w_o.shape[0] // dh
    h2 = h.reshape(bsz * seq_len, d_model)
    qkv = _norm_matmul(h2, norm_w, w_qkv.astype(BF16), BF16).reshape(bsz, seq_len, -1)
    q, k = _qk_prep(qkv, q_norm_w, k_norm_w, tables[0], tables[1], n_heads, LOG2E * dh ** -0.5)
    o = _attention(q, k, qkv, n_heads)
    out = _matmul_residual(o.reshape(bsz * seq_len, n_heads * dh), w_o.astype(BF16), h2)
    return out.reshape(bsz, seq_len, d_model)


def _ffn_layer(h, norm_w, w_up, conv_w, conv_b, w_down):
    bsz, seq_len, d_model = h.shape
    h2 = h.reshape(bsz * seq_len, d_model)
    f = w_down.shape[0]
    gated = _norm_matmul_conv(h2, norm_w, [w_up[:, :f].astype(BF16), w_up[:, f:].astype(BF16)],
                              [conv_w[:, :f], conv_w[:, f:]], [conv_b[:f], conv_b[f:]], seq_len)
    out = _matmul_residual(gated, w_down.astype(BF16), h2)
    return out.reshape(bsz, seq_len, d_model)


def kernel(x, norm_mix_w, norm_ffn_w, ssm_w_in, ssm_conv_w, ssm_conv_b, ssm_dt_bias, ssm_a_log, ssm_d, ssm_norm_w, ssm_w_out, att_w_qkv, att_q_norm_w, att_k_norm_w, att_w_o, ffn_w_up, ffn_conv_w, ffn_conv_b, ffn_w_down):
    depth = norm_mix_w.shape[0]
    tables = _rope_tables(x.shape[1], att_q_norm_w.shape[-1])
    h = x
    for i in range(depth):
        j = i // N_MIXERS
        if i % N_MIXERS == 0:
            h = _ssd_layer(h, norm_mix_w[i], ssm_w_in[j], ssm_conv_w[j], ssm_conv_b[j], ssm_dt_bias[j],
                           ssm_a_log[j], ssm_d[j], ssm_norm_w[j], ssm_w_out[j])
        else:
            h = _gqa_layer(h, norm_mix_w[i], att_w_qkv[j], att_q_norm_w[j], att_k_norm_w[j], att_w_o[j], tables)
        h = _ffn_layer(h, norm_ffn_w[i], ffn_w_up[i], ffn_conv_w[i], ffn_conv_b[i], ffn_w_down[i])
    return h
```
